```python
import math
import jax, jax.numpy as jnp
from jax import lax
import numpy as np

D_MODEL = 1024
BATCH = 8
SEQ = 2048
DEPTH = 4

N_EVEN = (DEPTH + 1) // 2
N_ODD = DEPTH // 2
EPS = 1e-6
NEG_INF = -1e30
Q_BLOCK = 128
FOX_WIDTH = D_MODEL // 2
FOX_HEAD_DIM = 64
FOX_HEADS = FOX_WIDTH // FOX_HEAD_DIM
SC_WIDTH = D_MODEL - FOX_WIDTH
SC_GROUPS = SC_WIDTH // 64
SC_K = 3
AB_IN = 3 * FOX_WIDTH + FOX_HEADS + 3 * SC_WIDTH
LRU_WIDTH = D_MODEL
LRU_BW = 256
LRU_BLOCKS = LRU_WIDTH // LRU_BW
RG_CONV_K = 4
RG_C = 8.0
RG_MIN_RAD = 0.9
RG_MAX_RAD = 0.999
MEM_LEN = 256
MEM_HEADS = 4
MEM_HEAD_DIM = D_MODEL // MEM_HEADS
D_FF = ((8 * D_MODEL // 3 + 255) // 256) * 256

kernel_name = "fox_shortconv_rglru_sandwich_hybrid"


def rmsnorm(x, g):
    x32 = x.astype(jnp.float32)
    y = x32 * lax.rsqrt(jnp.mean(x32 * x32, axis=-1, keepdims=True) + EPS)
    return y.astype(x.dtype) * g


def causal_depthwise_conv(x, w):
    k_width, ch = w.shape
    return lax.conv_general_dilated(
        x, w[:, None, :], window_strides=(1,), padding=[(k_width - 1, 0)],
        dimension_numbers=("NWC", "WIO", "NWC"), feature_group_count=ch)


def forgetting_attention(q, k, v, log_f):
    seq = q.shape[1]
    scale = q.shape[-1] ** -0.5
    cum = jnp.cumsum(log_f, axis=1).transpose(0, 2, 1)
    outs = []
    for blk in range(seq // Q_BLOCK):
        lo, hi = blk * Q_BLOCK, (blk + 1) * Q_BLOCK
        s = jnp.einsum("bqhd,bkhd->bhqk", q[:, lo:hi], k[:, :hi],
                       preferred_element_type=jnp.float32) * scale
        s = s + cum[:, :, lo:hi, None] - cum[:, :, None, :hi]
        causal = (lo + jnp.arange(Q_BLOCK))[:, None] >= jnp.arange(hi)[None, :]
        s = jnp.where(causal, s, NEG_INF)
        p = jax.nn.softmax(s, axis=-1).astype(v.dtype)
        outs.append(jnp.einsum("bhqk,bkhd->bqhd", p, v[:, :hi]))
    return jnp.concatenate(outs, axis=1)


def fox_shortconv_mixer(h, w_in, b_f, conv_w, w_out):
    bsz, seq, _ = h.shape
    proj = h @ w_in
    i1 = FOX_WIDTH
    i2 = 2 * FOX_WIDTH
    i3 = 3 * FOX_WIDTH
    i4 = i3 + FOX_HEADS
    i5 = i4 + SC_WIDTH
    i6 = i5 + SC_WIDTH
    q, k, v, f_logit, b_gate, c_gate, u = jnp.split(proj, [i1, i2, i3, i4, i5, i6], axis=-1)
    heads = lambda t: t.reshape(bsz, seq, FOX_HEADS, FOX_HEAD_DIM)
    log_f = jax.nn.log_sigmoid((f_logit + b_f).astype(jnp.float32))
    y_a = forgetting_attention(heads(q), heads(k), heads(v), log_f).reshape(bsz, seq, FOX_WIDTH)
    y_b = b_gate * causal_depthwise_conv(c_gate * u, conv_w)
    return jnp.concatenate([y_a, y_b], axis=-1) @ w_out


def _lru_combine(c1, c2):
    a1, b1 = c1
    a2, b2 = c2
    return a1 * a2, a2 * b1 + b2


def rglru_mixer(h, w_in, conv_w, conv_b, w_a, b_a, w_i, b_i, lam, w_out):
    bsz, seq, _ = h.shape
    gate, u = jnp.split(h @ w_in, 2, axis=-1)
    u = causal_depthwise_conv(u, conv_w) + conv_b
    ub = u.reshape(bsz, seq, LRU_BLOCKS, LRU_BW)
    r = jax.nn.sigmoid(jnp.einsum("bsnc,ncd->bsnd", ub, w_a) + b_a).reshape(bsz, seq, LRU_WIDTH)
    i = jax.nn.sigmoid(jnp.einsum("bsnc,ncd->bsnd", ub, w_i) + b_i).reshape(bsz, seq, LRU_WIDTH)
    log_a = -RG_C * r.astype(jnp.float32) * jax.nn.softplus(-lam.astype(jnp.float32))
    a = jnp.exp(log_a)
    b = jnp.sqrt(-jnp.expm1(2.0 * log_a)) * (i * u).astype(jnp.float32)
    _, hs = lax.associative_scan(_lru_combine, (a, b), axis=1)
    y = jax.nn.gelu(gate) * hs.astype(h.dtype)
    return y @ w_out


def memory_cross_attention(h, m, w_q, w_kv, w_o):
    bsz, seq, _ = h.shape
    mlen = m.shape[1]
    q = (h @ w_q).reshape(bsz, seq, MEM_HEADS, MEM_HEAD_DIM)
    k, v = jnp.split(m @ w_kv, 2, axis=-1)
    k = k.reshape(bsz, mlen, MEM_HEADS, MEM_HEAD_DIM)
    v = v.reshape(bsz, mlen, MEM_HEADS, MEM_HEAD_DIM)
    s = jnp.einsum("bqhd,bkhd->bhqk", q, k, preferred_element_type=jnp.float32) * (MEM_HEAD_DIM ** -0.5)
    p = jax.nn.softmax(s, axis=-1).astype(v.dtype)
    o = jnp.einsum("bhqk,bkhd->bqhd", p, v).reshape(bsz, seq, D_MODEL)
    return o @ w_o


def swiglu(h, w_gu, w_down):
    g, u = jnp.split(h @ w_gu, 2, axis=-1)
    return (jax.nn.silu(g) * u) @ w_down


def setup_inputs(seed: int = 0) -> dict:
    key = jax.random.key(seed)
    ks = iter(jax.random.split(key, 40))
    dense = lambda shape, fan_in: jax.random.normal(next(ks), shape, jnp.float32) * (fan_in ** -0.5)
    gain = lambda shape: 1.0 + 0.02 * jax.random.normal(next(ks), shape, jnp.float32)
    small = lambda shape: 0.01 * jax.random.normal(next(ks), shape, jnp.float32)
    x = jax.random.normal(next(ks), (BATCH, SEQ, D_MODEL), jnp.float32)
    mem = jax.random.normal(next(ks), (BATCH, MEM_LEN, D_MODEL), jnp.float32)
    rad = jax.random.uniform(next(ks), (N_ODD, LRU_WIDTH), jnp.float32, RG_MIN_RAD, RG_MAX_RAD)
    c_lam = -jnp.log(jnp.expm1(-jnp.log(rad) / RG_C))
    return {
        "x": x,
        "mem": mem,
        "g_mix_pre": gain((DEPTH, D_MODEL)),
        "g_mix_post": gain((DEPTH, D_MODEL)),
        "g_cross_pre": gain((DEPTH, D_MODEL)),
        "g_mem": gain((DEPTH, D_MODEL)),
        "g_cross_post": gain((DEPTH, D_MODEL)),
        "g_ffn_pre": gain((DEPTH, D_MODEL)),
        "g_ffn_post": gain((DEPTH, D_MODEL)),
        "w_xq": dense((DEPTH, D_MODEL, D_MODEL), D_MODEL),
        "w_xkv": dense((DEPTH, D_MODEL, 2 * D_MODEL), D_MODEL),
        "w_xo": dense((DEPTH, D_MODEL, D_MODEL), D_MODEL),
        "w_ffn_gu": dense((DEPTH, D_MODEL, 2 * D_FF), D_MODEL),
        "w_ffn_down": dense((DEPTH, D_FF, D_MODEL), D_FF),
        "ab_w_in": dense((N_EVEN, D_MODEL, AB_IN), D_MODEL),
        "ab_b_f": jax.random.uniform(next(ks), (N_EVEN, FOX_HEADS), jnp.float32, 2.0, 5.0),
        "ab_conv_w": dense((N_EVEN, SC_K, SC_WIDTH), SC_K),
        "ab_w_out": dense((N_EVEN, D_MODEL, D_MODEL), D_MODEL),
        "c_w_in": dense((N_ODD, D_MODEL, 2 * LRU_WIDTH), D_MODEL),
        "c_conv_w": dense((N_ODD, RG_CONV_K, LRU_WIDTH), RG_CONV_K),
        "c_conv_b": small((N_ODD, LRU_WIDTH)),
        "c_w_a": dense((N_ODD, LRU_BLOCKS, LRU_BW, LRU_BW), LRU_BW),
        "c_b_a": small((N_ODD, LRU_BLOCKS, LRU_BW)),
        "c_w_i": dense((N_ODD, LRU_BLOCKS, LRU_BW, LRU_BW), LRU_BW),
        "c_b_i": small((N_ODD, LRU_BLOCKS, LRU_BW)),
        "c_lam": c_lam,
        "c_w_out": dense((N_ODD, LRU_WIDTH, D_MODEL), LRU_WIDTH),
    }


def reference(x, mem, g_mix_pre, g_mix_post, g_cross_pre, g_mem, g_cross_post,
              g_ffn_pre, g_ffn_post, w_xq, w_xkv, w_xo, w_ffn_gu, w_ffn_down,
              ab_w_in, ab_b_f, ab_conv_w, ab_w_out,
              c_w_in, c_conv_w, c_conv_b, c_w_a, c_b_a, c_w_i, c_b_i, c_lam, c_w_out):
    for layer in range(DEPTH):
        h = rmsnorm(x, g_mix_pre[layer])
        if layer % 2 == 0:
            e = layer // 2
            y = fox_shortconv_mixer(h, ab_w_in[e], ab_b_f[e], ab_conv_w[e], ab_w_out[e])
        else:
            o = layer // 2
            y = rglru_mixer(h, c_w_in[o], c_conv_w[o], c_conv_b[o], c_w_a[o], c_b_a[o],
                            c_w_i[o], c_b_i[o], c_lam[o], c_w_out[o])
        x = x + rmsnorm(y, g_mix_post[layer])
        h = rmsnorm(x, g_cross_pre[layer])
        m = rmsnorm(mem, g_mem[layer])
        y = memory_cross_attention(h, m, w_xq[layer], w_xkv[layer], w_xo[layer])
        x = x + rmsnorm(y, g_cross_post[layer])
        h = rmsnorm(x, g_ffn_pre[layer])
        y = swiglu(h, w_ffn_gu[layer], w_ffn_down[layer])
        x = x + rmsnorm(y, g_ffn_post[layer])
    return x
```

```python
import functools
import math

import jax
import jax.numpy as jnp
from jax import lax
from jax.experimental import pallas as pl
from jax.experimental.pallas import tpu as pltpu

F32 = jnp.float32
BF16 = jnp.bfloat16

EPS = 1e-6
NEG_INF = -1e30
FOX_HEAD_DIM = 64
SC_K = 3
LRU_BW = 256
RG_CONV_K = 4
RG_C = 8.0
MEM_HEADS = 4

LANES = 128
SUBLANES = 8
HEAD_SLOT = LANES
VMEM_LIMIT = 56 * 1024 * 1024

ROW_TILE = 512
LRU_TILE = 256
ATT_TQ = 256
ATT_TK = 256
FFN_CHUNKS = ((0, 1024), (1024, 1024), (2048, 768))

NT_DIMS = (((1,), (1,)), ((), ()))


def _dot(a, b):
    return jnp.dot(a, b, preferred_element_type=F32)


def _dot_nt(a, b):
    return lax.dot_general(a, b, NT_DIMS, preferred_element_type=F32)


def _rms(x, g):
    ms = jnp.mean(x * x, axis=-1, keepdims=True)
    return x * lax.rsqrt(ms + EPS) * g


def _split3(x):
    hi = x.astype(BF16).astype(F32)
    r = x - hi
    mid = r.astype(BF16).astype(F32)
    lo = (r - mid).astype(BF16).astype(F32)
    return hi, mid, lo


def _causal_conv(cur, prev_tail, w_ref, k_width):
    t = cur.shape[0]
    head_cur = cur[0:SUBLANES, :]
    row8 = lax.broadcasted_iota(jnp.int32, head_cur.shape, 0)
    main = cur * w_ref[k_width - 1:k_width, :]
    head = head_cur * w_ref[k_width - 1:k_width, :]
    for d in range(1, k_width):
        wk = w_ref[k_width - 1 - d:k_width - d, :]
        main = main + pltpu.roll(cur, d, 0) * wk
        shifted = jnp.where(row8 < d, pltpu.roll(prev_tail, d, 0), pltpu.roll(head_cur, d, 0))
        head = head + shifted * wk
    del t
    return main, head


def _even_in_kernel(x_ref, g_ref, w_ref, bf_ref, cw_ref, tri_ref, pq_ref, pk_ref,
                    qa_ref, ka_ref, v_ref, yb_ref, carry_c, carry_cu, *,
                    tiles_per_seq, offs):
    o_q, o_k, o_v, o_f, o_b, o_c, o_u, o_end = offs
    tm = x_ref.shape[0]
    first = (pl.program_id(0) % tiles_per_seq) == 0

    @pl.when(first)
    def _():
        carry_c[...] = jnp.zeros_like(carry_c)
        carry_cu[...] = jnp.zeros_like(carry_cu)

    h = _rms(x_ref[...], g_ref[...]).astype(BF16)

    lane = lax.broadcasted_iota(jnp.int32, (tm, LANES), 1)
    fl = _dot(h, w_ref[:, o_f:o_b]) + bf_ref[...]
    lf = jax.nn.log_sigmoid(fl)
    hi, mid, lo = _split3(lf)
    parts = jnp.where(lane < 8, hi, jnp.where(lane < 16, mid, lo)).astype(BF16)
    r = _dot(tri_ref[...], parts)
    cum = r + pltpu.roll(r, LANES - 8, 1) + pltpu.roll(r, LANES - 16, 1)
    cum = jnp.where(lane < 8, cum + carry_c[...], 0.0)
    carry_c[...] = cum[tm - 1:tm, :]
    chi, cmid, clo = _split3(cum)
    sel = jnp.where(lane < 8, chi,
                    jnp.where(lane < 16, pltpu.roll(cmid, 8, 1),
                              jnp.where(lane < 24, pltpu.roll(clo, 16, 1),
                                        jnp.where(lane == 24, 1.0, 0.0)))).astype(BF16)

    scale = FOX_HEAD_DIM ** -0.5
    qa_ref[...] = (_dot(h, w_ref[:, o_q:o_k]) * scale + _dot(sel, pq_ref[...])).astype(BF16)
    ka_ref[...] = (_dot(h, w_ref[:, o_k:o_v]) + _dot(sel, pk_ref[...])).astype(BF16)
    v_ref[...] = _dot(h, w_ref[:, o_v:o_f]).astype(BF16)

    b_gate = _dot(h, w_ref[:, o_b:o_c])
    cu = _dot(h, w_ref[:, o_c:o_u]) * _dot(h, w_ref[:, o_u:o_end])
    main, head = _causal_conv(cu, carry_cu[...], cw_ref, SC_K)
    carry_cu[...] = cu[tm - SUBLANES:tm, :]
    yb_ref[...] = (b_gate * main).astype(BF16)
    yb_ref[0:SUBLANES, :] = (b_gate[0:SUBLANES, :] * head).astype(BF16)


def _fox_attn_kernel(qa_ref, ka_ref, v_ref, o_ref):
    tq = qa_ref.shape[0]
    qi = pl.program_id(2)
    lane = lax.broadcasted_iota(jnp.int32, (tq, LANES), 1)
    row = lax.broadcasted_iota(jnp.int32, (tq, ATT_TK), 0)
    col = lax.broadcasted_iota(jnp.int32, (tq, ATT_TK), 1)
    outs = []
    for hh in range(2):
        q = qa_ref[:, hh * HEAD_SLOT:(hh + 1) * HEAD_SLOT]

        def step(j, carry, masked):
            m, l, acc = carry
            k0 = pl.multiple_of(j * ATT_TK, ATT_TK)
            k = ka_ref[pl.ds(k0, ATT_TK), hh * HEAD_SLOT:(hh + 1) * HEAD_SLOT]
            v = v_ref[pl.ds(k0, ATT_TK), :]
            s = _dot_nt(q, k)
            if masked:
                s = jnp.where(row >= col, s, NEG_INF)
            m_new = jnp.maximum(m, jnp.max(s, axis=-1, keepdims=True))
            alpha = jnp.exp(m - m_new)
            p = jnp.exp(s - m_new)
            l = alpha * l + jnp.sum(p, axis=-1, keepdims=True)
            acc = alpha * acc + _dot(p.astype(BF16), v)
            return m_new, l, acc

        init = (jnp.full((tq, 1), NEG_INF, F32), jnp.zeros((tq, 1), F32),
                jnp.zeros((tq, LANES), F32))
        carry = lax.fori_loop(0, qi, functools.partial(step, masked=False), init)
        _, l, acc = step(qi, carry, True)
        outs.append(acc / l)
    o_ref[...] = jnp.where(lane < FOX_HEAD_DIM, outs[0], outs[1]).astype(BF16)


def _lru_kernel(x_ref, g_ref, win_ref, cw_ref, cb_ref, wa_ref, ba_ref, wi_ref, bi_ref,
                lam_ref, y_ref, carry_h, carry_u, conv_scr, *, tiles_per_seq):
    tm, width = x_ref.shape
    first = (pl.program_id(0) % tiles_per_seq) == 0

    @pl.when(first)
    def _():
        carry_h[...] = jnp.zeros_like(carry_h)
        carry_u[...] = jnp.zeros_like(carry_u)

    h = _rms(x_ref[...], g_ref[...]).astype(BF16)
    u = _dot(h, win_ref[:, width:2 * width])
    main, head = _causal_conv(u, carry_u[...], cw_ref, RG_CONV_K)
    carry_u[...] = u[tm - SUBLANES:tm, :]
    conv_scr[...] = main + cb_ref[...]
    conv_scr[0:SUBLANES, :] = head + cb_ref[...]
    uc = conv_scr[...]
    ucb = uc.astype(BF16)

    nb = width // LRU_BW
    r = jnp.concatenate(
        [_dot(ucb[:, n * LRU_BW:(n + 1) * LRU_BW], wa_ref[n]) for n in range(nb)], axis=-1)
    r = jax.nn.sigmoid(r + ba_ref[...])
    gi = jnp.concatenate(
        [_dot(ucb[:, n * LRU_BW:(n + 1) * LRU_BW], wi_ref[n]) for n in range(nb)], axis=-1)
    gi = jax.nn.sigmoid(gi + bi_ref[...])

    log_a = -RG_C * r * jax.nn.softplus(-lam_ref[...])
    a = jnp.exp(log_a)
    b = jnp.sqrt(-jnp.tanh(log_a) * (1.0 + a * a)) * (gi * uc)

    row = lax.broadcasted_iota(jnp.int32, (tm, width), 0)
    d = 1
    while d < tm:
        keep = row >= d
        a_sh = jnp.where(keep, pltpu.roll(a, d, 0), 1.0)
        b_sh = jnp.where(keep, pltpu.roll(b, d, 0), 0.0)
        b = a * b_sh + b
        a = a * a_sh
        d *= 2
    hs = b + a * carry_h[...]
    carry_h[...] = hs[tm - 1:tm, :]

    gate = _dot(h, win_ref[:, 0:width])
    y_ref[...] = (jax.nn.gelu(gate) * hs).astype(BF16)


def _mem_kv_kernel(mem_ref, g_ref, w_ref, k_ref, v_ref):
    d_model = mem_ref.shape[-1]
    m = _rms(mem_ref[...], g_ref[...]).astype(BF16)
    k_ref[...] = _dot(m, w_ref[:, 0:d_model]).astype(BF16)
    v_ref[...] = _dot(m, w_ref[:, d_model:2 * d_model]).astype(BF16)


def _cross_kernel(*refs, n_parts):
    y_refs = refs[0:n_parts]
    w_refs = refs[n_parts:2 * n_parts]
    (x_ref, gpost_ref, gcpre_ref, wq_ref, k_ref, v_ref, wo_ref, gcpost_ref, gfpre_ref,
     xo_ref, ho_ref) = refs[2 * n_parts:]
    d_model = x_ref.shape[-1]
    hd = d_model // MEM_HEADS

    y = _dot(y_refs[0][...], w_refs[0][...])
    for p in range(1, n_parts):
        y = y + _dot(y_refs[p][...], w_refs[p][...])
    x1 = x_ref[...] + _rms(y, gpost_ref[...])

    h = _rms(x1, gcpre_ref[...]).astype(BF16)
    q = (_dot(h, wq_ref[...]) * (hd ** -0.5)).astype(BF16)
    outs = []
    for n in range(MEM_HEADS):
        s = _dot_nt(q[:, n * hd:(n + 1) * hd], k_ref[:, n * hd:(n + 1) * hd])
        e = jnp.exp(s - jnp.max(s, axis=-1, keepdims=True))
        l = jnp.sum(e, axis=-1, keepdims=True)
        o = _dot(e.astype(BF16), v_ref[:, n * hd:(n + 1) * hd]) / l
        outs.append(o.astype(BF16))
    y2 = _dot(jnp.concatenate(outs, axis=-1), wo_ref[...])
    x2 = x1 + _rms(y2, gcpost_ref[...])
    xo_ref[...] = x2
    ho_ref[...] = _rms(x2, gfpre_ref[...]).astype(BF16)


def _ffn_kernel(h_ref, x_ref, wg_ref, wu_ref, wd_ref, g_ref, xo_ref):
    h = h_ref[...]
    acc = None
    for c0, cw in FFN_CHUNKS:
        a = (jax.nn.silu(_dot(h, wg_ref[:, c0:c0 + cw])) * _dot(h, wu_ref[:, c0:c0 + cw]))
        part = _dot(a.astype(BF16), wd_ref[c0:c0 + cw, :])
        acc = part if acc is None else acc + part
    xo_ref[...] = x_ref[...] + _rms(acc, g_ref[...])


def _params(sem):
    return pltpu.CompilerParams(dimension_semantics=sem, vmem_limit_bytes=VMEM_LIMIT)


def _const_spec(shape, index):
    return pl.BlockSpec(shape, index, pipeline_mode=pl.Buffered(1))


def _layer_vec_spec(layer, width):
    return _const_spec((None, 1, width), lambda *_: (layer, 0, 0))


def _even_in(x2, g_pre, layer, w_all, bf_pad, conv_w, e, tri, pq, pk, seq, offs):
    n, d_model = x2.shape
    tm = ROW_TILE
    fox_w = offs[3] - offs[2]
    sc_w = offs[5] - offs[4]
    n_heads = fox_w // FOX_HEAD_DIM
    row = lambda i: (i, 0)
    kern = functools.partial(_even_in_kernel, tiles_per_seq=seq // tm, offs=offs)
    return pl.pallas_call(
        kern,
        grid=(n // tm,),
        in_specs=[
            pl.BlockSpec((tm, d_model), row),
            _layer_vec_spec(layer, d_model),
            _const_spec((None, d_model, offs[-1]), lambda i: (e, 0, 0)),
            _const_spec((None, 1, LANES), lambda i: (e, 0, 0)),
            _const_spec((None, SC_K, sc_w), lambda i: (e, 0, 0)),
            _const_spec((tm, tm), lambda i: (0, 0)),
            _const_spec((LANES, n_heads * HEAD_SLOT), lambda i: (0, 0)),
            _const_spec((LANES, n_heads * HEAD_SLOT), lambda i: (0, 0)),
        ],
        out_specs=[
            pl.BlockSpec((tm, n_heads * HEAD_SLOT), row),
            pl.BlockSpec((tm, n_heads * HEAD_SLOT), row),
            pl.BlockSpec((tm, fox_w), row),
            pl.BlockSpec((tm, sc_w), row),
        ],
        out_shape=[
            jax.ShapeDtypeStruct((n, n_heads * HEAD_SLOT), BF16),
            jax.ShapeDtypeStruct((n, n_heads * HEAD_SLOT), BF16),
            jax.ShapeDtypeStruct((n, fox_w), BF16),
            jax.ShapeDtypeStruct((n, sc_w), BF16),
        ],
        scratch_shapes=[pltpu.VMEM((1, LANES), F32), pltpu.VMEM((SUBLANES, sc_w), F32)],
        compiler_params=_params(("arbitrary",)),
        name="even_in",
    )(x2, g_pre, w_all, bf_pad, conv_w, tri, pq, pk)


def _fox_attn(qa, ka, v, bsz, seq):
    n = qa.shape[0]
    n_pairs = v.shape[1] // LANES
    nq = seq // ATT_TQ
    return pl.pallas_call(
        _fox_attn_kernel,
        grid=(bsz, n_pairs, nq),
        in_specs=[
            pl.BlockSpec((ATT_TQ, 2 * HEAD_SLOT), lambda b, p, i: (b * nq + i, p)),
            pl.BlockSpec((seq, 2 * HEAD_SLOT), lambda b, p, i: (b, p)),
            pl.BlockSpec((seq, LANES), lambda b, p, i: (b, p)),
        ],
        out_specs=pl.BlockSpec((ATT_TQ, LANES), lambda b, p, i: (b * nq + i, p)),
        out_shape=jax.ShapeDtypeStruct((n, v.shape[1]), BF16),
        compiler_params=_params(("arbitrary", "arbitrary", "arbitrary")),
        name="fox_attn",
    )(qa, ka, v)


def _lru(x2, g_pre, layer, w_in, conv_w, conv_b, w_a, b_a, w_i, b_i, lam, o, seq):
    n, d_model = x2.shape
    tm = LRU_TILE
    nb = d_model // LRU_BW
    kern = functools.partial(_lru_kernel, tiles_per_seq=seq // tm)
    vec = lambda: _const_spec((None, 1, d_model), lambda i: (o, 0, 0))
    return pl.pallas_call(
        kern,
        grid=(n // tm,),
        in_specs=[
            pl.BlockSpec((tm, d_model), lambda i: (i, 0)),
            _layer_vec_spec(layer, d_model),
            _const_spec((None, d_model, 2 * d_model), lambda i: (o, 0, 0)),
            _const_spec((None, RG_CONV_K, d_model), lambda i: (o, 0, 0)),
            vec(),
            _const_spec((None, nb, LRU_BW, LRU_BW), lambda i: (o, 0, 0, 0)),
            vec(),
            _const_spec((None, nb, LRU_BW, LRU_BW), lambda i: (o, 0, 0, 0)),
            vec(),
            vec(),
        ],
        out_specs=pl.BlockSpec((tm, d_model), lambda i: (i, 0)),
        out_shape=jax.ShapeDtypeStruct((n, d_model), BF16),
        scratch_shapes=[pltpu.VMEM((1, d_model), F32), pltpu.VMEM((SUBLANES, d_model), F32),
                        pltpu.VMEM((tm, d_model), F32)],
        compiler_params=_params(("arbitrary",)),
        name="rglru",
    )(x2, g_pre, w_in, conv_w, conv_b, w_a, b_a, w_i, b_i, lam)


def _mem_kv(mem, g_mem, w_xkv):
    bsz, mlen, d_model = mem.shape
    depth = w_xkv.shape[0]
    out = jax.ShapeDtypeStruct((depth, bsz, mlen, d_model), BF16)
    return pl.pallas_call(
        _mem_kv_kernel,
        grid=(depth, bsz),
        in_specs=[
            pl.BlockSpec((None, mlen, d_model), lambda l, b: (b, 0, 0)),
            pl.BlockSpec((None, 1, d_model), lambda l, b: (l, 0, 0)),
            pl.BlockSpec((None, d_model, 2 * d_model), lambda l, b: (l, 0, 0)),
        ],
        out_specs=[
            pl.BlockSpec((None, None, mlen, d_model), lambda l, b: (l, b, 0, 0)),
            pl.BlockSpec((None, None, mlen, d_model), lambda l, b: (l, b, 0, 0)),
        ],
        out_shape=[out, out],
        compiler_params=_params(("arbitrary", "arbitrary")),
        name="mem_kv",
    )(mem, g_mem, w_xkv)


def _cross(y_parts, w_parts, x2, layer, g_post, g_cpre, w_xq, k_mem, v_mem, w_xo,
           g_cpost, g_fpre, seq):
    n, d_model = x2.shape
    tm = ROW_TILE
    tiles_per_seq = seq // tm
    mlen = k_mem.shape[2]
    row = lambda i: (i, 0)
    in_specs = [pl.BlockSpec((tm, yp.shape[1]), row) for yp in y_parts]
    w_arrays = []
    for yp, (w_arr, idx, blk) in zip(y_parts, w_parts):
        in_specs.append(_const_spec((None, yp.shape[1], d_model),
                                    lambda i, idx=idx, blk=blk: (idx, blk, 0)))
        w_arrays.append(w_arr)
    mem_spec = lambda: pl.BlockSpec((None, None, mlen, d_model),
                                    lambda i: (layer, i // tiles_per_seq, 0, 0))
    in_specs += [
        pl.BlockSpec((tm, d_model), row),
        _layer_vec_spec(layer, d_model),
        _layer_vec_spec(layer, d_model),
        _const_spec((None, d_model, d_model), lambda i: (layer, 0, 0)),
        mem_spec(),
        mem_spec(),
        _const_spec((None, d_model, d_model), lambda i: (layer, 0, 0)),
        _layer_vec_spec(layer, d_model),
        _layer_vec_spec(layer, d_model),
    ]
    kern = functools.partial(_cross_kernel, n_parts=len(y_parts))
    return pl.pallas_call(
        kern,
        grid=(n // tm,),
        in_specs=in_specs,
        out_specs=[pl.BlockSpec((tm, d_model), row), pl.BlockSpec((tm, d_model), row)],
        out_shape=[jax.ShapeDtypeStruct((n, d_model), F32),
                   jax.ShapeDtypeStruct((n, d_model), BF16)],
        compiler_params=_params(("arbitrary",)),
        name="cross",
    )(*y_parts, *w_arrays, x2, g_post, g_cpre, w_xq, k_mem, v_mem, w_xo, g_cpost, g_fpre)


def _ffn(h, x2, layer, w_gu, w_down, g_post):
    n, d_model = x2.shape
    d_ff = w_down.shape[1]
    tm = ROW_TILE
    row = lambda i: (i, 0)
    return pl.pallas_call(
        _ffn_kernel,
        grid=(n // tm,),
        in_specs=[
            pl.BlockSpec((tm, d_model), row),
            pl.BlockSpec((tm, d_model), row),
            _const_spec((None, d_model, d_ff), lambda i: (layer, 0, 0)),
            _const_spec((None, d_model, d_ff), lambda i: (layer, 0, 1)),
            _const_spec((None, d_ff, d_model), lambda i: (layer, 0, 0)),
            _layer_vec_spec(layer, d_model),
        ],
        out_specs=pl.BlockSpec((tm, d_model), row),
        out_shape=jax.ShapeDtypeStruct((n, d_model), F32),
        compiler_params=_params(("arbitrary",)),
        name="ffn",
    )(h, x2, w_gu, w_gu, w_down, g_post)


def _pad_heads(w, n_heads):
    e, d, _ = w.shape
    w = w.reshape(e, d, n_heads, FOX_HEAD_DIM)
    w = jnp.pad(w, ((0, 0), (0, 0), (0, 0), (0, HEAD_SLOT - FOX_HEAD_DIM)))
    return w.reshape(e, d, n_heads * HEAD_SLOT)


def _bias_placement(n_heads):
    pq = [[0.0] * (n_heads * HEAD_SLOT) for _ in range(LANES)]
    pk = [[0.0] * (n_heads * HEAD_SLOT) for _ in range(LANES)]
    one_lane = 24
    for h in range(n_heads):
        base = h * HEAD_SLOT + FOX_HEAD_DIM
        for part in range(3):
            pq[part * 8 + h][base + part] = 1.0
            pk[one_lane][base + part] = 1.0
            pq[one_lane][base + 3 + part] = 1.0
            pk[part * 8 + h][base + 3 + part] = -1.0
    return jnp.array(pq, BF16), jnp.array(pk, BF16)


def kernel(x, mem, g_mix_pre, g_mix_post, g_cross_pre, g_mem, g_cross_post, g_ffn_pre, g_ffn_post, w_xq, w_xkv, w_xo, w_ffn_gu, w_ffn_down, ab_w_in, ab_b_f, ab_conv_w, ab_w_out, c_w_in, c_conv_w, c_conv_b, c_w_a, c_b_a, c_w_i, c_b_i, c_lam, c_w_out):
    bsz, seq, d_model = x.shape
    depth = g_mix_pre.shape[0]
    n = bsz * seq
    n_heads = ab_b_f.shape[1]
    fox_w = n_heads * FOX_HEAD_DIM
    sc_w = ab_conv_w.shape[2]
    assert n_heads <= 8 and seq % ROW_TILE == 0 and seq % LRU_TILE == 0 and seq % ATT_TQ == 0

    vec3 = lambda g: g.reshape(g.shape[0], 1, g.shape[-1])
    g_mix_pre, g_mix_post, g_cross_pre, g_mem, g_cross_post, g_ffn_pre, g_ffn_post = map(
        vec3, (g_mix_pre, g_mix_post, g_cross_pre, g_mem, g_cross_post, g_ffn_pre, g_ffn_post))

    i1, i2, i3 = fox_w, 2 * fox_w, 3 * fox_w
    i4 = i3 + n_heads
    i5, i6 = i4 + sc_w, i4 + 2 * sc_w
    w_f = ab_w_in[:, :, i3:i4]
    w_f3 = jnp.pad(jnp.concatenate([w_f, w_f, w_f], axis=-1),
                   ((0, 0), (0, 0), (0, LANES - 3 * n_heads)))
    pieces = [_pad_heads(ab_w_in[:, :, 0:i1], n_heads), _pad_heads(ab_w_in[:, :, i1:i2], n_heads),
              ab_w_in[:, :, i2:i3], w_f3, ab_w_in[:, :, i4:i5], ab_w_in[:, :, i5:i6],
              ab_w_in[:, :, i6:]]
    offs = [0]
    for p in pieces:
        offs.append(offs[-1] + p.shape[-1])
    offs = tuple(offs)
    w_even = jnp.concatenate(pieces, axis=-1).astype(BF16)
    b_f3 = jnp.pad(jnp.concatenate([ab_b_f, ab_b_f, ab_b_f], axis=-1),
                   ((0, 0), (0, LANES - 3 * n_heads))).reshape(-1, 1, LANES)
    tri = (lax.broadcasted_iota(jnp.int32, (ROW_TILE, ROW_TILE), 0)
           >= lax.broadcasted_iota(jnp.int32, (ROW_TILE, ROW_TILE), 1)).astype(BF16)
    pq, pk = _bias_placement(n_heads)

    w_out_even = ab_w_out.astype(BF16)
    w_c_in = c_w_in.astype(BF16)
    w_c_out = c_w_out.astype(BF16)
    w_a = c_w_a.astype(BF16)
    w_i = c_w_i.astype(BF16)
    b_a = c_b_a.reshape(c_b_a.shape[0], 1, -1)
    b_i = c_b_i.reshape(c_b_i.shape[0], 1, -1)
    conv_b = vec3(c_conv_b)
    lam = vec3(c_lam)
    w_xq_b = w_xq.astype(BF16)
    w_xo_b = w_xo.astype(BF16)
    w_gu_b = w_ffn_gu.astype(BF16)
    w_down_b = w_ffn_down.astype(BF16)

    k_mem, v_mem = _mem_kv(mem, g_mem, w_xkv.astype(BF16))

    x2 = x.reshape(n, d_model)
    for layer in range(depth):
        if layer % 2 == 0:
            e = layer // 2
            qa, ka, v, yb = _even_in(x2, g_mix_pre, layer, w_even, b_f3, ab_conv_w, e,
                                     tri, pq, pk, seq, offs)
            ya = _fox_attn(qa, ka, v, bsz, seq)
            y_parts = [ya, yb]
            assert fox_w == sc_w
            w_parts = [(w_out_even, e, 0), (w_out_even, e, 1)]
        else:
            o = layer // 2
            ym = _lru(x2, g_mix_pre, layer, w_c_in, c_conv_w, conv_b, w_a, b_a, w_i, b_i,
                      lam, o, seq)
            y_parts = [ym]
            w_parts = [(w_c_out, o, 0)]
        x2, h = _cross(y_parts, w_parts, x2, layer, g_mix_post, g_cross_pre, w_xq_b,
                       k_mem, v_mem, w_xo_b, g_cross_post, g_ffn_pre, seq)
        x2 = _ffn(h, x2, layer, w_gu_b, w_down_b, g_ffn_post)
    return x2.reshape(bsz, seq, d_model)
```

```python
import functools
import math

import jax
import jax.numpy as jnp
from jax import lax
from jax.experimental import pallas as pl
from jax.experimental.pallas import tpu as pltpu

F32 = jnp.float32
BF16 = jnp.bfloat16

EPS = 1e-6
NEG_INF = -1e30
FOX_HEAD_DIM = 64
SC_K = 3
LRU_BW = 256
RG_CONV_K = 4
RG_C = 8.0
MEM_HEADS = 4

LANES = 128
SUBLANES = 8
HEAD_SLOT = LANES
VT_SLOT = 80
LOG2_E = math.log2(math.e)
VMEM_LIMIT = 56 * 1024 * 1024

ROW_TILE = 512
LRU_TILE = 256
ATT_TQ = 256
ATT_TK = 256
ATT_HEADS_PER_STEP = 8
FFN_CHUNKS = ((0, 1024), (1024, 1024), (2048, 768))

NT_DIMS = (((1,), (1,)), ((), ()))


def _dot(a, b):
    return jnp.dot(a, b, preferred_element_type=F32)


def _dot_nt(a, b):
    return lax.dot_general(a, b, NT_DIMS, preferred_element_type=F32)


def _rms(x, g):
    ms = jnp.mean(x * x, axis=-1, keepdims=True)
    return x * lax.rsqrt(ms + EPS) * g


def _split3(x):
    hi = x.astype(BF16).astype(F32)
    r = x - hi
    mid = r.astype(BF16).astype(F32)
    lo = (r - mid).astype(BF16).astype(F32)
    return hi, mid, lo


def _causal_conv(cur, prev_tail, w_ref, k_width):
    t = cur.shape[0]
    head_cur = cur[0:SUBLANES, :]
    row8 = lax.broadcasted_iota(jnp.int32, head_cur.shape, 0)
    main = cur * w_ref[k_width - 1:k_width, :]
    head = head_cur * w_ref[k_width - 1:k_width, :]
    for d in range(1, k_width):
        wk = w_ref[k_width - 1 - d:k_width - d, :]
        main = main + pltpu.roll(cur, d, 0) * wk
        shifted = jnp.where(row8 < d, pltpu.roll(prev_tail, d, 0), pltpu.roll(head_cur, d, 0))
        head = head + shifted * wk
    del t
    return main, head


def _even_in_kernel(x_ref, g_ref, w_ref, wt_ref, vbias_ref, bf_ref, cw_ref, tri_ref,
                    pqt_ref, pk_ref, qt_ref, ka_ref, vt_ref, yb_ref, carry_c, carry_cu, *,
                    tiles_per_seq, offs):
    o_k, o_f, o_b, o_c, o_u, o_end = offs
    tm = x_ref.shape[0]
    first = (pl.program_id(0) % tiles_per_seq) == 0

    @pl.when(first)
    def _():
        carry_c[...] = jnp.zeros_like(carry_c)
        carry_cu[...] = jnp.zeros_like(carry_cu)

    h = _rms(x_ref[...], g_ref[...]).astype(BF16)

    lane = lax.broadcasted_iota(jnp.int32, (tm, LANES), 1)
    fl = _dot(h, w_ref[:, o_f:o_b]) + bf_ref[...]
    lf = jax.nn.log_sigmoid(fl)
    hi, mid, lo = _split3(lf)
    parts = jnp.where(lane < 8, hi, jnp.where(lane < 16, mid, lo)).astype(BF16)
    r = _dot(tri_ref[...], parts)
    cum = r + pltpu.roll(r, LANES - 8, 1) + pltpu.roll(r, LANES - 16, 1)
    cum = jnp.where(lane < 8, cum + carry_c[...], 0.0)
    carry_c[...] = cum[tm - 1:tm, :]
    chi, cmid, clo = _split3(cum * LOG2_E)
    sel = jnp.where(lane < 8, chi,
                    jnp.where(lane < 16, pltpu.roll(cmid, 8, 1),
                              jnp.where(lane < 24, pltpu.roll(clo, 16, 1),
                                        jnp.where(lane == 24, 1.0, 0.0)))).astype(BF16)

    scale = FOX_HEAD_DIM ** -0.5 * LOG2_E
    ka_ref[...] = (_dot(h, w_ref[:, o_k:o_f]) + _dot(sel, pk_ref[...])).astype(BF16)
    q_rows = qt_ref.shape[1]
    qt = _dot_nt(wt_ref[0:q_rows, :], h) * scale + _dot_nt(pqt_ref[...], sel)
    for c in range(qt_ref.shape[0]):
        w = qt_ref.shape[-1]
        qt_ref[c] = qt[:, c * w:(c + 1) * w].astype(BF16)
    vt = _dot_nt(wt_ref[q_rows:, :], h) + vbias_ref[...]
    for c in range(vt_ref.shape[0]):
        w = vt_ref.shape[-1]
        vt_ref[c] = vt[:, c * w:(c + 1) * w].astype(BF16)

    b_gate = _dot(h, w_ref[:, o_b:o_c])
    cu = _dot(h, w_ref[:, o_c:o_u]) * _dot(h, w_ref[:, o_u:o_end])
    main, head = _causal_conv(cu, carry_cu[...], cw_ref, SC_K)
    carry_cu[...] = cu[tm - SUBLANES:tm, :]
    yb_ref[...] = (b_gate * main).astype(BF16)
    yb_ref[0:SUBLANES, :] = (b_gate[0:SUBLANES, :] * head).astype(BF16)


def _fox_attn_kernel(qt_ref, ka_ref, vt_ref, o_ref, acc_ref):
    tq = qt_ref.shape[-1]
    qi = pl.program_id(2)
    krow = lax.broadcasted_iota(jnp.int32, (ATT_TK, tq), 0)
    qcol = lax.broadcasted_iota(jnp.int32, (ATT_TK, tq), 1)
    heads = range(acc_ref.shape[0])

    def step(j, ms, masked):
        k0 = pl.multiple_of(j * ATT_TK, ATT_TK)
        sts = [_dot(ka_ref[pl.ds(k0, ATT_TK), hh * HEAD_SLOT:(hh + 1) * HEAD_SLOT],
                    qt_ref[hh * HEAD_SLOT:(hh + 1) * HEAD_SLOT, :]) for hh in heads]
        ps, alphas, new = [], [], []
        for hh in heads:
            st = sts[hh]
            if masked:
                st = jnp.where(qcol >= krow, st, NEG_INF)
            m_new = jnp.maximum(ms[hh], jnp.max(st, axis=0, keepdims=True))
            alphas.append(jnp.exp2(ms[hh] - m_new))
            ps.append(jnp.exp2(st - m_new).astype(BF16))
            new.append(m_new)
        for hh in heads:
            vt = vt_ref[j, hh * VT_SLOT:(hh + 1) * VT_SLOT, :]
            acc_ref[hh] = alphas[hh] * acc_ref[hh] + _dot(vt, ps[hh])
        return tuple(new)

    acc_ref[...] = jnp.zeros_like(acc_ref)
    ms = tuple(jnp.full((1, tq), NEG_INF, F32) for _ in heads)
    ms = lax.fori_loop(0, qi, functools.partial(step, masked=False), ms)
    step(qi, ms, True)
    o_t = jnp.concatenate(
        [acc_ref[hh, 0:FOX_HEAD_DIM, :] / acc_ref[hh, FOX_HEAD_DIM:FOX_HEAD_DIM + 1, :]
         for hh in heads], axis=0)
    o_ref[...] = o_t.T.astype(BF16)


def _lru_kernel(x_ref, g_ref, win_ref, cw_ref, cb_ref, wa_ref, ba_ref, wi_ref, bi_ref,
                lam_ref, y_ref, carry_h, carry_u, conv_scr, *, tiles_per_seq):
    tm, width = x_ref.shape
    first = (pl.program_id(0) % tiles_per_seq) == 0

    @pl.when(first)
    def _():
        carry_h[...] = jnp.zeros_like(carry_h)
        carry_u[...] = jnp.zeros_like(carry_u)

    h = _rms(x_ref[...], g_ref[...]).astype(BF16)
    u = _dot(h, win_ref[:, width:2 * width])
    main, head = _causal_conv(u, carry_u[...], cw_ref, RG_CONV_K)
    carry_u[...] = u[tm - SUBLANES:tm, :]
    conv_scr[...] = main + cb_ref[...]
    conv_scr[0:SUBLANES, :] = head + cb_ref[...]
    uc = conv_scr[...]
    ucb = uc.astype(BF16)

    nb = width // LRU_BW
    r = jnp.concatenate(
        [_dot(ucb[:, n * LRU_BW:(n + 1) * LRU_BW], wa_ref[n]) for n in range(nb)], axis=-1)
    r = jax.nn.sigmoid(r + ba_ref[...])
    gi = jnp.concatenate(
        [_dot(ucb[:, n * LRU_BW:(n + 1) * LRU_BW], wi_ref[n]) for n in range(nb)], axis=-1)
    gi = jax.nn.sigmoid(gi + bi_ref[...])

    log_a = -RG_C * r * jax.nn.softplus(-lam_ref[...])
    a = jnp.exp(log_a)
    b = jnp.sqrt(-jnp.tanh(log_a) * (1.0 + a * a)) * (gi * uc)

    row = lax.broadcasted_iota(jnp.int32, (tm, width), 0)
    d = 1
    while d < tm:
        keep = row >= d
        a_sh = jnp.where(keep, pltpu.roll(a, d, 0), 1.0)
        b_sh = jnp.where(keep, pltpu.roll(b, d, 0), 0.0)
        b = a * b_sh + b
        a = a * a_sh
        d *= 2
    hs = b + a * carry_h[...]
    carry_h[...] = hs[tm - 1:tm, :]

    gate = _dot(h, win_ref[:, 0:width])
    y_ref[...] = (jax.nn.gelu(gate) * hs).astype(BF16)


def _mem_kv_kernel(mem_ref, g_ref, w_ref, k_ref, v_ref):
    d_model = mem_ref.shape[-1]
    m = _rms(mem_ref[...], g_ref[...]).astype(BF16)
    k_ref[...] = _dot(m, w_ref[:, 0:d_model]).astype(BF16)
    v_ref[...] = _dot(m, w_ref[:, d_model:2 * d_model]).astype(BF16)


def _cross_kernel(*refs, n_parts):
    y_refs = refs[0:n_parts]
    w_refs = refs[n_parts:2 * n_parts]
    (x_ref, gpost_ref, gcpre_ref, wq_ref, k_ref, v_ref, wo_ref, gcpost_ref, gfpre_ref,
     xo_ref, ho_ref) = refs[2 * n_parts:]
    d_model = x_ref.shape[-1]
    hd = d_model // MEM_HEADS

    y = _dot(y_refs[0][...], w_refs[0][...])
    for p in range(1, n_parts):
        y = y + _dot(y_refs[p][...], w_refs[p][...])
    x1 = x_ref[...] + _rms(y, gpost_ref[...])

    h = _rms(x1, gcpre_ref[...]).astype(BF16)
    q = (_dot(h, wq_ref[...]) * (hd ** -0.5)).astype(BF16)
    outs = []
    for n in range(MEM_HEADS):
        s = _dot_nt(q[:, n * hd:(n + 1) * hd], k_ref[:, n * hd:(n + 1) * hd])
        e = jnp.exp(s - jnp.max(s, axis=-1, keepdims=True))
        l = jnp.sum(e, axis=-1, keepdims=True)
        o = _dot(e.astype(BF16), v_ref[:, n * hd:(n + 1) * hd]) / l
        outs.append(o.astype(BF16))
    y2 = _dot(jnp.concatenate(outs, axis=-1), wo_ref[...])
    x2 = x1 + _rms(y2, gcpost_ref[...])
    xo_ref[...] = x2
    ho_ref[...] = _rms(x2, gfpre_ref[...]).astype(BF16)


def _ffn_kernel(h_ref, x_ref, wg_ref, wu_ref, wd_ref, g_ref, xo_ref):
    h = h_ref[...]
    acc = None
    for c0, cw in FFN_CHUNKS:
        a = (jax.nn.silu(_dot(h, wg_ref[:, c0:c0 + cw])) * _dot(h, wu_ref[:, c0:c0 + cw]))
        part = _dot(a.astype(BF16), wd_ref[c0:c0 + cw, :])
        acc = part if acc is None else acc + part
    xo_ref[...] = x_ref[...] + _rms(acc, g_ref[...])


def _params(sem):
    return pltpu.CompilerParams(dimension_semantics=sem, vmem_limit_bytes=VMEM_LIMIT)


def _const_spec(shape, index):
    return pl.BlockSpec(shape, index, pipeline_mode=pl.Buffered(1))


def _layer_vec_spec(layer, width):
    return _const_spec((None, 1, width), lambda *_: (layer, 0, 0))


def _even_in(x2, g_pre, layer, w_all, w_t, v_bias, bf_pad, conv_w, e, tri, pqt, pk, seq, offs):
    n, d_model = x2.shape
    tm = ROW_TILE
    sc_w = offs[3] - offs[2]
    q_rows = pqt.shape[0]
    n_heads = q_rows // HEAD_SLOT
    vt_rows = n_heads * VT_SLOT
    tiles_per_seq = seq // tm
    row = lambda i: (i, 0)
    chunked = lambda i: (i // tiles_per_seq, i % tiles_per_seq, 0, 0)
    kern = functools.partial(_even_in_kernel, tiles_per_seq=tiles_per_seq, offs=offs)
    return pl.pallas_call(
        kern,
        grid=(n // tm,),
        in_specs=[
            pl.BlockSpec((tm, d_model), row),
            _layer_vec_spec(layer, d_model),
            _const_spec((None, d_model, offs[-1]), lambda i: (e, 0, 0)),
            _const_spec((None, q_rows + vt_rows, d_model), lambda i: (e, 0, 0)),
            _const_spec((vt_rows, 1), lambda i: (0, 0)),
            _const_spec((None, 1, LANES), lambda i: (e, 0, 0)),
            _const_spec((None, SC_K, sc_w), lambda i: (e, 0, 0)),
            _const_spec((tm, tm), lambda i: (0, 0)),
            _const_spec((q_rows, LANES), lambda i: (0, 0)),
            _const_spec((LANES, q_rows), lambda i: (0, 0)),
        ],
        out_specs=[
            pl.BlockSpec((None, tm // ATT_TQ, q_rows, ATT_TQ), chunked),
            pl.BlockSpec((tm, q_rows), row),
            pl.BlockSpec((None, tm // ATT_TK, vt_rows, ATT_TK), chunked),
            pl.BlockSpec((tm, sc_w), row),
        ],
        out_shape=[
            jax.ShapeDtypeStruct((n // seq, seq // ATT_TQ, q_rows, ATT_TQ), BF16),
            jax.ShapeDtypeStruct((n, q_rows), BF16),
            jax.ShapeDtypeStruct((n // seq, seq // ATT_TK, vt_rows, ATT_TK), BF16),
            jax.ShapeDtypeStruct((n, sc_w), BF16),
        ],
        scratch_shapes=[pltpu.VMEM((1, LANES), F32), pltpu.VMEM((SUBLANES, sc_w), F32)],
        compiler_params=_params(("arbitrary",)),
        name="even_in",
    )(x2, g_pre, w_all, w_t, v_bias, bf_pad, conv_w, tri, pqt, pk)


def _fox_attn(qt, ka, vt, seq):
    bsz, nq, _, _ = qt.shape
    _, nkb, vt_rows, _ = vt.shape
    hps = ATT_HEADS_PER_STEP
    n_groups = vt_rows // (hps * VT_SLOT)
    return pl.pallas_call(
        _fox_attn_kernel,
        grid=(bsz, n_groups, nq),
        in_specs=[
            pl.BlockSpec((None, None, hps * HEAD_SLOT, ATT_TQ), lambda b, p, i: (b, i, p, 0)),
            pl.BlockSpec((seq, hps * HEAD_SLOT), lambda b, p, i: (b, p)),
            pl.BlockSpec((None, nkb, hps * VT_SLOT, ATT_TK), lambda b, p, i: (b, 0, p, 0)),
        ],
        out_specs=pl.BlockSpec((ATT_TQ, hps * FOX_HEAD_DIM), lambda b, p, i: (b * nq + i, p)),
        out_shape=jax.ShapeDtypeStruct((bsz * seq, n_groups * hps * FOX_HEAD_DIM), BF16),
        scratch_shapes=[pltpu.VMEM((hps, VT_SLOT, ATT_TQ), F32)],
        compiler_params=_params(("arbitrary", "arbitrary", "arbitrary")),
        name="fox_attn",
    )(qt, ka, vt)


def _lru(x2, g_pre, layer, w_in, conv_w, conv_b, w_a, b_a, w_i, b_i, lam, o, seq):
    n, d_model = x2.shape
    tm = LRU_TILE
    nb = d_model // LRU_BW
    kern = functools.partial(_lru_kernel, tiles_per_seq=seq // tm)
    vec = lambda: _const_spec((None, 1, d_model), lambda i: (o, 0, 0))
    return pl.pallas_call(
        kern,
        grid=(n // tm,),
        in_specs=[
            pl.BlockSpec((tm, d_model), lambda i: (i, 0)),
            _layer_vec_spec(layer, d_model),
            _const_spec((None, d_model, 2 * d_model), lambda i: (o, 0, 0)),
            _const_spec((None, RG_CONV_K, d_model), lambda i: (o, 0, 0)),
            vec(),
            _const_spec((None, nb, LRU_BW, LRU_BW), lambda i: (o, 0, 0, 0)),
            vec(),
            _const_spec((None, nb, LRU_BW, LRU_BW), lambda i: (o, 0, 0, 0)),
            vec(),
            vec(),
        ],
        out_specs=pl.BlockSpec((tm, d_model), lambda i: (i, 0)),
        out_shape=jax.ShapeDtypeStruct((n, d_model), BF16),
        scratch_shapes=[pltpu.VMEM((1, d_model), F32), pltpu.VMEM((SUBLANES, d_model), F32),
                        pltpu.VMEM((tm, d_model), F32)],
        compiler_params=_params(("arbitrary",)),
        name="rglru",
    )(x2, g_pre, w_in, conv_w, conv_b, w_a, b_a, w_i, b_i, lam)


def _mem_kv(mem, g_mem, w_xkv):
    bsz, mlen, d_model = mem.shape
    depth = w_xkv.shape[0]
    out = jax.ShapeDtypeStruct((depth, bsz, mlen, d_model), BF16)
    return pl.pallas_call(
        _mem_kv_kernel,
        grid=(depth, bsz),
        in_specs=[
            pl.BlockSpec((None, mlen, d_model), lambda l, b: (b, 0, 0)),
            pl.BlockSpec((None, 1, d_model), lambda l, b: (l, 0, 0)),
            pl.BlockSpec((None, d_model, 2 * d_model), lambda l, b: (l, 0, 0)),
        ],
        out_specs=[
            pl.BlockSpec((None, None, mlen, d_model), lambda l, b: (l, b, 0, 0)),
            pl.BlockSpec((None, None, mlen, d_model), lambda l, b: (l, b, 0, 0)),
        ],
        out_shape=[out, out],
        compiler_params=_params(("arbitrary", "arbitrary")),
        name="mem_kv",
    )(mem, g_mem, w_xkv)


def _cross(y_parts, w_parts, x2, layer, g_post, g_cpre, w_xq, k_mem, v_mem, w_xo,
           g_cpost, g_fpre, seq):
    n, d_model = x2.shape
    tm = ROW_TILE
    tiles_per_seq = seq // tm
    mlen = k_mem.shape[2]
    row = lambda i: (i, 0)
    in_specs = [pl.BlockSpec((tm, yp.shape[1]), row) for yp in y_parts]
    w_arrays = []
    for yp, (w_arr, idx, blk) in zip(y_parts, w_parts):
        in_specs.append(_const_spec((None, yp.shape[1], d_model),
                                    lambda i, idx=idx, blk=blk: (idx, blk, 0)))
        w_arrays.append(w_arr)
    mem_spec = lambda: pl.BlockSpec((None, None, mlen, d_model),
                                    lambda i: (layer, i // tiles_per_seq, 0, 0))
    in_specs += [
        pl.BlockSpec((tm, d_model), row),
        _layer_vec_spec(layer, d_model),
        _layer_vec_spec(layer, d_model),
        _const_spec((None, d_model, d_model), lambda i: (layer, 0, 0)),
        mem_spec(),
        mem_spec(),
        _const_spec((None, d_model, d_model), lambda i: (layer, 0, 0)),
        _layer_vec_spec(layer, d_model),
        _layer_vec_spec(layer, d_model),
    ]
    kern = functools.partial(_cross_kernel, n_parts=len(y_parts))
    return pl.pallas_call(
        kern,
        grid=(n // tm,),
        in_specs=in_specs,
        out_specs=[pl.BlockSpec((tm, d_model), row), pl.BlockSpec((tm, d_model), row)],
        out_shape=[jax.ShapeDtypeStruct((n, d_model), F32),
                   jax.ShapeDtypeStruct((n, d_model), BF16)],
        compiler_params=_params(("arbitrary",)),
        name="cross",
    )(*y_parts, *w_arrays, x2, g_post, g_cpre, w_xq, k_mem, v_mem, w_xo, g_cpost, g_fpre)


def _ffn(h, x2, layer, w_gu, w_down, g_post):
    n, d_model = x2.shape
    d_ff = w_down.shape[1]
    tm = ROW_TILE
    row = lambda i: (i, 0)
    return pl.pallas_call(
        _ffn_kernel,
        grid=(n // tm,),
        in_specs=[
            pl.BlockSpec((tm, d_model), row),
            pl.BlockSpec((tm, d_model), row),
            _const_spec((None, d_model, d_ff), lambda i: (layer, 0, 0)),
            _const_spec((None, d_model, d_ff), lambda i: (layer, 0, 1)),
            _const_spec((None, d_ff, d_model), lambda i: (layer, 0, 0)),
            _layer_vec_spec(layer, d_model),
        ],
        out_specs=pl.BlockSpec((tm, d_model), row),
        out_shape=jax.ShapeDtypeStruct((n, d_model), F32),
        compiler_params=_params(("arbitrary",)),
        name="ffn",
    )(h, x2, w_gu, w_gu, w_down, g_post)


def _pad_heads(w, n_heads, slot):
    e, d, _ = w.shape
    w = w.reshape(e, d, n_heads, FOX_HEAD_DIM)
    w = jnp.pad(w, ((0, 0), (0, 0), (0, 0), (0, slot - FOX_HEAD_DIM)))
    return w.reshape(e, d, n_heads * slot)


def _bias_placement(n_heads):
    pq = [[0.0] * (n_heads * HEAD_SLOT) for _ in range(LANES)]
    pk = [[0.0] * (n_heads * HEAD_SLOT) for _ in range(LANES)]
    one_lane = 24
    for h in range(n_heads):
        base = h * HEAD_SLOT + FOX_HEAD_DIM
        for part in range(3):
            pq[part * 8 + h][base + part] = 1.0
            pk[one_lane][base + part] = 1.0
            pq[one_lane][base + 3 + part] = 1.0
            pk[part * 8 + h][base + 3 + part] = -1.0
    return jnp.array(pq, BF16), jnp.array(pk, BF16)


def kernel(x, mem, g_mix_pre, g_mix_post, g_cross_pre, g_mem, g_cross_post, g_ffn_pre, g_ffn_post, w_xq, w_xkv, w_xo, w_ffn_gu, w_ffn_down, ab_w_in, ab_b_f, ab_conv_w, ab_w_out, c_w_in, c_conv_w, c_conv_b, c_w_a, c_b_a, c_w_i, c_b_i, c_lam, c_w_out):
    bsz, seq, d_model = x.shape
    depth = g_mix_pre.shape[0]
    n = bsz * seq
    n_heads = ab_b_f.shape[1]
    fox_w = n_heads * FOX_HEAD_DIM
    sc_w = ab_conv_w.shape[2]
    assert n_heads <= 8 and seq % ROW_TILE == 0 and seq % LRU_TILE == 0 and seq % ATT_TQ == 0

    vec3 = lambda g: g.reshape(g.shape[0], 1, g.shape[-1])
    g_mix_pre, g_mix_post, g_cross_pre, g_mem, g_cross_post, g_ffn_pre, g_ffn_post = map(
        vec3, (g_mix_pre, g_mix_post, g_cross_pre, g_mem, g_cross_post, g_ffn_pre, g_ffn_post))

    i1, i2, i3 = fox_w, 2 * fox_w, 3 * fox_w
    i4 = i3 + n_heads
    i5, i6 = i4 + sc_w, i4 + 2 * sc_w
    w_f = ab_w_in[:, :, i3:i4]
    w_f3 = jnp.pad(jnp.concatenate([w_f, w_f, w_f], axis=-1),
                   ((0, 0), (0, 0), (0, LANES - 3 * n_heads)))
    pieces = [_pad_heads(ab_w_in[:, :, i1:i2], n_heads, HEAD_SLOT),
              w_f3, ab_w_in[:, :, i4:i5], ab_w_in[:, :, i5:i6], ab_w_in[:, :, i6:]]
    w_t = jnp.concatenate([_pad_heads(ab_w_in[:, :, 0:i1], n_heads, HEAD_SLOT),
                           _pad_heads(ab_w_in[:, :, i2:i3], n_heads, VT_SLOT)], axis=-1)
    w_t = w_t.transpose(0, 2, 1).astype(BF16)
    v_bias = (jnp.arange(n_heads * VT_SLOT) % VT_SLOT == FOX_HEAD_DIM).astype(F32)[:, None]
    offs = [0]
    for p in pieces:
        offs.append(offs[-1] + p.shape[-1])
    offs = tuple(offs)
    w_even = jnp.concatenate(pieces, axis=-1).astype(BF16)
    b_f3 = jnp.pad(jnp.concatenate([ab_b_f, ab_b_f, ab_b_f], axis=-1),
                   ((0, 0), (0, LANES - 3 * n_heads))).reshape(-1, 1, LANES)
    tri = (lax.broadcasted_iota(jnp.int32, (ROW_TILE, ROW_TILE), 0)
           >= lax.broadcasted_iota(jnp.int32, (ROW_TILE, ROW_TILE), 1)).astype(BF16)
    pq, pk = _bias_placement(n_heads)
    pqt = pq.T

    w_out_even = ab_w_out.astype(BF16)
    w_c_in = c_w_in.astype(BF16)
    w_c_out = c_w_out.astype(BF16)
    w_a = c_w_a.astype(BF16)
    w_i = c_w_i.astype(BF16)
    b_a = c_b_a.reshape(c_b_a.shape[0], 1, -1)
    b_i = c_b_i.reshape(c_b_i.shape[0], 1, -1)
    conv_b = vec3(c_conv_b)
    lam = vec3(c_lam)
    w_xq_b = w_xq.astype(BF16)
    w_xo_b = w_xo.astype(BF16)
    w_gu_b = w_ffn_gu.astype(BF16)
    w_down_b = w_ffn_down.astype(BF16)

    k_mem, v_mem = _mem_kv(mem, g_mem, w_xkv.astype(BF16))

    x2 = x.reshape(n, d_model)
    for layer in range(depth):
        if layer % 2 == 0:
            e = layer // 2
            qt, ka, vt, yb = _even_in(x2, g_mix_pre, layer, w_even, w_t, v_bias, b_f3,
                                      ab_conv_w, e, tri, pqt, pk, seq, offs)
            ya = _fox_attn(qt, ka, vt, seq)
            y_parts = [ya, yb]
            assert fox_w == sc_w
            w_parts = [(w_out_even, e, 0), (w_out_even, e, 1)]
        else:
            o = layer // 2
            ym = _lru(x2, g_mix_pre, layer, w_c_in, c_conv_w, conv_b, w_a, b_a, w_i, b_i,
                      lam, o, seq)
            y_parts = [ym]
            w_parts = [(w_c_out, o, 0)]
        x2, h = _cross(y_parts, w_parts, x2, layer, g_mix_post, g_cross_pre, w_xq_b,
                       k_mem, v_mem, w_xo_b, g_cross_post, g_ffn_pre, seq)
        x2 = _ffn(h, x2, layer, w_gu_b, w_down_b, g_ffn_post)
    return x2.reshape(bsz, seq, d_model)
```

```python
import functools
import math

import jax
import jax.numpy as jnp
from jax import lax
from jax.experimental import pallas as pl
from jax.experimental.pallas import tpu as pltpu

F32 = jnp.float32
BF16 = jnp.bfloat16

EPS = 1e-6
NEG_INF = -1e30
FOX_HEAD_DIM = 64
SC_K = 3
LRU_BW = 256
RG_CONV_K = 4
RG_C = 8.0
MEM_HEADS = 4

LANES = 128
SUBLANES = 8
HEAD_SLOT = LANES
VT_SLOT = 80
F_LANE0 = FOX_HEAD_DIM + 6
LOG2_E = math.log2(math.e)
VMEM_LIMIT = 56 * 1024 * 1024

ROW_TILE = 512
CROSS_TILE = 1024
CROSS_SPLIT = 4
LRU_STEPS = 64
ATT_TQ = 256
ATT_TK = 256
ATT_HEADS_PER_STEP = 8
FFN_CHUNKS = ((0, 1024), (1024, 1024), (2048, 768))

NT_DIMS = (((1,), (1,)), ((), ()))


def _dot(a, b):
    return jnp.dot(a, b, preferred_element_type=F32)


def _dot_nt(a, b):
    return lax.dot_general(a, b, NT_DIMS, preferred_element_type=F32)


def _rms(x, g):
    ms = jnp.mean(x * x, axis=-1, keepdims=True)
    return x * lax.rsqrt(ms + EPS) * g


def _split3(x):
    hi = x.astype(BF16).astype(F32)
    r = x - hi
    mid = r.astype(BF16).astype(F32)
    lo = (r - mid).astype(BF16).astype(F32)
    return hi, mid, lo


def _causal_conv(cur, prev_tail, w_ref, k_width):
    t = cur.shape[0]
    head_cur = cur[0:SUBLANES, :]
    row8 = lax.broadcasted_iota(jnp.int32, head_cur.shape, 0)
    main = cur * w_ref[k_width - 1:k_width, :]
    head = head_cur * w_ref[k_width - 1:k_width, :]
    for d in range(1, k_width):
        wk = w_ref[k_width - 1 - d:k_width - d, :]
        main = main + pltpu.roll(cur, d, 0) * wk
        shifted = jnp.where(row8 < d, pltpu.roll(prev_tail, d, 0), pltpu.roll(head_cur, d, 0))
        head = head + shifted * wk
    del t
    return main, head


def _even_in_kernel(x_ref, g_ref, w_ref, wt_ref, bf_ref, cw_ref, tri_ref,
                    pqt_ref, pk_ref, qt_ref, ka_ref, vt_ref, yb_ref, carry_c, carry_cu, *,
                    tiles_per_seq, offs):
    o_b, o_c, o_u, o_end = offs
    tm = x_ref.shape[0]
    fox_w = wt_ref.shape[0] // 2
    n_heads = fox_w // FOX_HEAD_DIM
    first = (pl.program_id(0) % tiles_per_seq) == 0

    @pl.when(first)
    def _():
        carry_c[...] = jnp.zeros_like(carry_c)
        carry_cu[...] = jnp.zeros_like(carry_cu)

    h = _rms(x_ref[...], g_ref[...]).astype(BF16)

    lane = lax.broadcasted_iota(jnp.int32, (tm, LANES), 1)
    kp = _dot(h, w_ref[:, 0:o_b])
    fl = pltpu.roll(kp[:, 0:LANES], LANES - F_LANE0, 1)
    fl = jnp.where(lane < 24, fl, 0.0) + bf_ref[...]

    lf = jax.nn.log_sigmoid(fl)
    hi, mid, lo = _split3(lf)
    parts = jnp.where(lane < 8, hi, jnp.where(lane < 16, mid, lo)).astype(BF16)
    r = _dot(tri_ref[...], parts)
    cum = r + pltpu.roll(r, LANES - 8, 1) + pltpu.roll(r, LANES - 16, 1)
    cum = jnp.where(lane < 8, cum + carry_c[...], 0.0)
    carry_c[...] = cum[tm - 1:tm, :]
    chi, cmid, clo = _split3(cum * LOG2_E)
    sel = jnp.where(lane < 8, chi,
                    jnp.where(lane < 16, pltpu.roll(cmid, 8, 1),
                              jnp.where(lane < 24, pltpu.roll(clo, 16, 1),
                                        jnp.where(lane == 24, 1.0, 0.0)))).astype(BF16)

    k0 = jnp.where(lane < FOX_HEAD_DIM, kp[:, 0:LANES], 0.0)
    kp = jnp.concatenate([k0, kp[:, LANES:]], axis=-1)
    ka_ref[...] = (kp + _dot(sel, pk_ref[...])).astype(BF16)

    scale = FOX_HEAD_DIM ** -0.5 * LOG2_E
    qt = _dot_nt(wt_ref[0:fox_w, :], h) * scale
    qb = _dot_nt(pqt_ref[...], sel)
    vt = _dot_nt(wt_ref[fox_w:, :], h)
    pad_rows = VT_SLOT - FOX_HEAD_DIM
    q_zero = jnp.zeros((HEAD_SLOT - FOX_HEAD_DIM - pad_rows, tm), F32)
    ones_blk = (lax.broadcasted_iota(jnp.int32, (pad_rows, tm), 0) == 0).astype(F32)
    q_pieces, v_pieces = [], []
    for n in range(n_heads):
        feat = slice(n * FOX_HEAD_DIM, (n + 1) * FOX_HEAD_DIM)
        q_pieces += [qt[feat, :], qb[n * pad_rows:(n + 1) * pad_rows, :], q_zero]
        v_pieces += [vt[feat, :], ones_blk]
    qt = jnp.concatenate(q_pieces, axis=0).astype(BF16)
    vt = jnp.concatenate(v_pieces, axis=0).astype(BF16)
    for c in range(qt_ref.shape[0]):
        w = qt_ref.shape[-1]
        qt_ref[c] = qt[:, c * w:(c + 1) * w]
    for c in range(vt_ref.shape[0]):
        w = vt_ref.shape[-1]
        vt_ref[c] = vt[:, c * w:(c + 1) * w]

    b_gate = _dot(h, w_ref[:, o_b:o_c])
    cu = _dot(h, w_ref[:, o_c:o_u]) * _dot(h, w_ref[:, o_u:o_end])
    main, head = _causal_conv(cu, carry_cu[...], cw_ref, SC_K)
    carry_cu[...] = cu[tm - SUBLANES:tm, :]
    yb_ref[...] = (b_gate * main).astype(BF16)
    yb_ref[0:SUBLANES, :] = (b_gate[0:SUBLANES, :] * head).astype(BF16)


def _fox_attn_kernel(qt_ref, ka_ref, vt_ref, o_ref, acc_ref):
    tq = qt_ref.shape[-1]
    qi = pl.program_id(2)
    krow = lax.broadcasted_iota(jnp.int32, (ATT_TK, tq), 0)
    qcol = lax.broadcasted_iota(jnp.int32, (ATT_TK, tq), 1)
    heads = range(acc_ref.shape[0])

    def step(j, ms, masked):
        k0 = pl.multiple_of(j * ATT_TK, ATT_TK)
        sts = [_dot(ka_ref[pl.ds(k0, ATT_TK), hh * HEAD_SLOT:(hh + 1) * HEAD_SLOT],
                    qt_ref[hh * HEAD_SLOT:(hh + 1) * HEAD_SLOT, :]) for hh in heads]
        ps, alphas, new = [], [], []
        for hh in heads:
            st = sts[hh]
            if masked:
                st = jnp.where(qcol >= krow, st, NEG_INF)
            m_new = jnp.maximum(ms[hh], jnp.max(st, axis=0, keepdims=True))
            alphas.append(jnp.exp2(ms[hh] - m_new))
            ps.append(jnp.exp2(st - m_new).astype(BF16))
            new.append(m_new)
        for hh in heads:
            vt = vt_ref[j, hh * VT_SLOT:(hh + 1) * VT_SLOT, :]
            acc_ref[hh] = alphas[hh] * acc_ref[hh] + _dot(vt, ps[hh])
        return tuple(new)

    acc_ref[...] = jnp.zeros_like(acc_ref)
    ms = tuple(jnp.full((1, tq), NEG_INF, F32) for _ in heads)
    ms = lax.fori_loop(0, qi, functools.partial(step, masked=False), ms)
    step(qi, ms, True)
    o_t = jnp.concatenate(
        [acc_ref[hh, 0:FOX_HEAD_DIM, :] / acc_ref[hh, FOX_HEAD_DIM:FOX_HEAD_DIM + 1, :]
         for hh in heads], axis=0)
    o_ref[...] = o_t.T.astype(BF16)


def _lru_kernel(x_ref, g_ref, win_ref, cw_ref, cb_ref, wa_ref, ba_ref, wi_ref, bi_ref,
                lam_ref, y_ref, carry_h, carry_u):
    bsz = x_ref.shape[0]
    width = g_ref.shape[-1]
    steps = x_ref.shape[1] // width
    tm = steps * bsz
    tail = (RG_CONV_K - 1) * bsz

    @pl.when(pl.program_id(0) == 0)
    def _():
        carry_h[...] = jnp.zeros_like(carry_h)
        carry_u[...] = jnp.zeros_like(carry_u)

    x = jnp.concatenate([x_ref[:, t * width:(t + 1) * width] for t in range(steps)], axis=0)
    h = _rms(x, g_ref[...]).astype(BF16)
    u = _dot(h, win_ref[:, width:2 * width])

    ext = jnp.concatenate([carry_u[...], u], axis=0)
    carry_u[...] = u[tm - tail:tm, :]
    uc = cb_ref[...] + u * cw_ref[RG_CONV_K - 1:RG_CONV_K, :]
    for d in range(1, RG_CONV_K):
        shifted = ext[tail - d * bsz:tail - d * bsz + tm, :]
        uc = uc + shifted * cw_ref[RG_CONV_K - 1 - d:RG_CONV_K - d, :]
    ucb = uc.astype(BF16)

    nb = width // LRU_BW
    r = jnp.concatenate(
        [_dot(ucb[:, n * LRU_BW:(n + 1) * LRU_BW], wa_ref[n]) for n in range(nb)], axis=-1)
    r = jax.nn.sigmoid(r + ba_ref[...])
    gi = jnp.concatenate(
        [_dot(ucb[:, n * LRU_BW:(n + 1) * LRU_BW], wi_ref[n]) for n in range(nb)], axis=-1)
    gi = jax.nn.sigmoid(gi + bi_ref[...])

    log_a = r * (-RG_C * jax.nn.softplus(-lam_ref[...]))
    a = jnp.exp(log_a)
    b = jnp.sqrt(-jnp.tanh(log_a) * (1.0 + a * a)) * (gi * uc)

    state = carry_h[...]
    hs = []
    for t in range(steps):
        state = a[t * bsz:(t + 1) * bsz, :] * state + b[t * bsz:(t + 1) * bsz, :]
        hs.append(state)
    carry_h[...] = state
    hs = jnp.concatenate(hs, axis=0)

    y = jax.nn.gelu(_dot(h, win_ref[:, 0:width])) * hs
    for t in range(steps):
        y_ref[:, t * width:(t + 1) * width] = y[t * bsz:(t + 1) * bsz, :].astype(BF16)


def _mem_kv_kernel(mem_ref, g_ref, wkt_ref, wv_ref, kt_ref, v_ref):
    m = _rms(mem_ref[...], g_ref[...]).astype(BF16)
    kt_ref[...] = _dot_nt(wkt_ref[...], m).astype(BF16)
    v_ref[...] = _dot(m, wv_ref[...]).astype(BF16)


def _cross_kernel(*refs, n_parts):
    y_refs = refs[0:n_parts]
    w_refs = refs[n_parts:2 * n_parts]
    (x_ref, gpost_ref, gcpre_ref, wq_ref, kt_ref, v_ref, wo_ref, gcpost_ref, gfpre_ref,
     xo_ref, ho_ref) = refs[2 * n_parts:]
    tm, d_model = x_ref.shape
    hd = d_model // MEM_HEADS
    sub = tm // CROSS_SPLIT
    rows = [pl.ds(s * sub, sub) for s in range(CROSS_SPLIT)]
    each = lambda f, *lists: [f(*args) for args in zip(*lists)]

    def mix_out(r):
        y = _dot(y_refs[0][r, :], w_refs[0][...])
        for p in range(1, n_parts):
            y = y + _dot(y_refs[p][r, :], w_refs[p][...])
        return y

    ys = each(mix_out, rows)
    x1s = each(lambda r, y: x_ref[r, :] + _rms(y, gpost_ref[...]), rows, ys)
    hs = each(lambda x1: _rms(x1, gcpre_ref[...]).astype(BF16), x1s)
    qs = each(lambda h: (_dot(h, wq_ref[...]) * (hd ** -0.5)).astype(BF16), hs)
    outs = [[] for _ in rows]
    for n in range(MEM_HEADS):
        ss = each(lambda q: _dot(q[:, n * hd:(n + 1) * hd], kt_ref[n * hd:(n + 1) * hd, :]), qs)
        es = each(lambda s: jnp.exp(s - jnp.max(s, axis=-1, keepdims=True)), ss)
        os_ = each(lambda e: _dot(e.astype(BF16), v_ref[:, n * hd:(n + 1) * hd])
                   / jnp.sum(e, axis=-1, keepdims=True), es)
        for lst, o in zip(outs, os_):
            lst.append(o.astype(BF16))
    y2s = each(lambda o: _dot(jnp.concatenate(o, axis=-1), wo_ref[...]), outs)
    x2s = each(lambda x1, y2: x1 + _rms(y2, gcpost_ref[...]), x1s, y2s)
    for r, x2 in zip(rows, x2s):
        xo_ref[r, :] = x2
        ho_ref[r, :] = _rms(x2, gfpre_ref[...]).astype(BF16)


def _ffn_kernel(h_ref, x_ref, wg_ref, wu_ref, wd_ref, g_ref, xo_ref):
    h = h_ref[...]
    acc = None
    for c0, cw in FFN_CHUNKS:
        a = (jax.nn.silu(_dot(h, wg_ref[:, c0:c0 + cw])) * _dot(h, wu_ref[:, c0:c0 + cw]))
        part = _dot(a.astype(BF16), wd_ref[c0:c0 + cw, :])
        acc = part if acc is None else acc + part
    xo_ref[...] = x_ref[...] + _rms(acc, g_ref[...])


def _params(sem):
    return pltpu.CompilerParams(dimension_semantics=sem, vmem_limit_bytes=VMEM_LIMIT)


def _const_spec(shape, index):
    return pl.BlockSpec(shape, index, pipeline_mode=pl.Buffered(1))


def _layer_vec_spec(layer, width):
    return _const_spec((None, 1, width), lambda *_: (layer, 0, 0))


def _even_in(x2, g_pre, layer, w_all, w_t, bf_pad, conv_w, e, tri, pqt, pk, seq, offs):
    n, d_model = x2.shape
    tm = ROW_TILE
    sc_w = offs[2] - offs[1]
    q_rows = pk.shape[1]
    n_heads = q_rows // HEAD_SLOT
    vt_rows = n_heads * VT_SLOT
    tiles_per_seq = seq // tm
    row = lambda i: (i, 0)
    chunked = lambda i: (i // tiles_per_seq, i % tiles_per_seq, 0, 0)
    kern = functools.partial(_even_in_kernel, tiles_per_seq=tiles_per_seq, offs=offs)
    return pl.pallas_call(
        kern,
        grid=(n // tm,),
        in_specs=[
            pl.BlockSpec((tm, d_model), row),
            _layer_vec_spec(layer, d_model),
            _const_spec((None, d_model, offs[-1]), lambda i: (e, 0, 0)),
            _const_spec((None, w_t.shape[1], d_model), lambda i: (e, 0, 0)),
            _const_spec((None, 1, LANES), lambda i: (e, 0, 0)),
            _const_spec((None, SC_K, sc_w), lambda i: (e, 0, 0)),
            _const_spec((tm, tm), lambda i: (0, 0)),
            _const_spec(pqt.shape, lambda i: (0, 0)),
            _const_spec((LANES, q_rows), lambda i: (0, 0)),
        ],
        out_specs=[
            pl.BlockSpec((None, tm // ATT_TQ, q_rows, ATT_TQ), chunked),
            pl.BlockSpec((tm, q_rows), row),
            pl.BlockSpec((None, tm // ATT_TK, vt_rows, ATT_TK), chunked),
            pl.BlockSpec((tm, sc_w), row),
        ],
        out_shape=[
            jax.ShapeDtypeStruct((n // seq, seq // ATT_TQ, q_rows, ATT_TQ), BF16),
            jax.ShapeDtypeStruct((n, q_rows), BF16),
            jax.ShapeDtypeStruct((n // seq, seq // ATT_TK, vt_rows, ATT_TK), BF16),
            jax.ShapeDtypeStruct((n, sc_w), BF16),
        ],
        scratch_shapes=[pltpu.VMEM((1, LANES), F32), pltpu.VMEM((SUBLANES, sc_w), F32)],
        compiler_params=_params(("arbitrary",)),
        name="even_in",
    )(x2, g_pre, w_all, w_t, bf_pad, conv_w, tri, pqt, pk)


def _fox_attn(qt, ka, vt, seq):
    bsz, nq, _, _ = qt.shape
    _, nkb, vt_rows, _ = vt.shape
    hps = ATT_HEADS_PER_STEP
    n_groups = vt_rows // (hps * VT_SLOT)
    return pl.pallas_call(
        _fox_attn_kernel,
        grid=(bsz, n_groups, nq),
        in_specs=[
            pl.BlockSpec((None, None, hps * HEAD_SLOT, ATT_TQ), lambda b, p, i: (b, i, p, 0)),
            pl.BlockSpec((seq, hps * HEAD_SLOT), lambda b, p, i: (b, p)),
            pl.BlockSpec((None, nkb, hps * VT_SLOT, ATT_TK), lambda b, p, i: (b, 0, p, 0)),
        ],
        out_specs=pl.BlockSpec((ATT_TQ, hps * FOX_HEAD_DIM), lambda b, p, i: (b * nq + i, p)),
        out_shape=jax.ShapeDtypeStruct((bsz * seq, n_groups * hps * FOX_HEAD_DIM), BF16),
        scratch_shapes=[pltpu.VMEM((hps, VT_SLOT, ATT_TQ), F32)],
        compiler_params=_params(("arbitrary", "arbitrary", "arbitrary")),
        name="fox_attn",
    )(qt, ka, vt)


def _lru(x2, g_pre, layer, w_in, conv_w, conv_b, w_a, b_a, w_i, b_i, lam, o, seq):
    n, d_model = x2.shape
    bsz = n // seq
    assert bsz == SUBLANES
    steps = LRU_STEPS
    nb = d_model // LRU_BW
    vec = lambda: _const_spec((None, 1, d_model), lambda i: (o, 0, 0))
    y = pl.pallas_call(
        _lru_kernel,
        grid=(seq // steps,),
        in_specs=[
            pl.BlockSpec((bsz, steps * d_model), lambda i: (0, i)),
            _layer_vec_spec(layer, d_model),
            _const_spec((None, d_model, 2 * d_model), lambda i: (o, 0, 0)),
            _const_spec((None, RG_CONV_K, d_model), lambda i: (o, 0, 0)),
            vec(),
            _const_spec((None, nb, LRU_BW, LRU_BW), lambda i: (o, 0, 0, 0)),
            vec(),
            _const_spec((None, nb, LRU_BW, LRU_BW), lambda i: (o, 0, 0, 0)),
            vec(),
            vec(),
        ],
        out_specs=pl.BlockSpec((bsz, steps * d_model), lambda i: (0, i)),
        out_shape=jax.ShapeDtypeStruct((bsz, seq * d_model), BF16),
        scratch_shapes=[pltpu.VMEM((bsz, d_model), F32),
                        pltpu.VMEM(((RG_CONV_K - 1) * bsz, d_model), F32)],
        compiler_params=_params(("arbitrary",)),
        name="rglru",
    )(x2.reshape(bsz, seq * d_model), g_pre, w_in, conv_w, conv_b, w_a, b_a, w_i, b_i, lam)
    return y.reshape(n, d_model)


def _mem_kv(mem, g_mem, w_kt, w_xkv):
    bsz, mlen, d_model = mem.shape
    depth = w_xkv.shape[0]
    return pl.pallas_call(
        _mem_kv_kernel,
        grid=(depth, bsz),
        in_specs=[
            pl.BlockSpec((None, mlen, d_model), lambda l, b: (b, 0, 0)),
            pl.BlockSpec((None, 1, d_model), lambda l, b: (l, 0, 0)),
            pl.BlockSpec((None, d_model, d_model), lambda l, b: (l, 0, 0)),
            pl.BlockSpec((None, d_model, d_model), lambda l, b: (l, 0, 1)),
        ],
        out_specs=[
            pl.BlockSpec((None, None, d_model, mlen), lambda l, b: (l, b, 0, 0)),
            pl.BlockSpec((None, None, mlen, d_model), lambda l, b: (l, b, 0, 0)),
        ],
        out_shape=[jax.ShapeDtypeStruct((depth, bsz, d_model, mlen), BF16),
                   jax.ShapeDtypeStruct((depth, bsz, mlen, d_model), BF16)],
        compiler_params=_params(("arbitrary", "arbitrary")),
        name="mem_kv",
    )(mem, g_mem, w_kt, w_xkv)


def _cross(y_parts, w_parts, x2, layer, g_post, g_cpre, w_xq, k_mem, v_mem, w_xo,
           g_cpost, g_fpre, seq):
    n, d_model = x2.shape
    tm = CROSS_TILE
    tiles_per_seq = seq // tm
    mlen = v_mem.shape[2]
    row = lambda i: (i, 0)
    in_specs = [pl.BlockSpec((tm, yp.shape[1]), row) for yp in y_parts]
    w_arrays = []
    for yp, (w_arr, idx, blk) in zip(y_parts, w_parts):
        in_specs.append(_const_spec((None, yp.shape[1], d_model),
                                    lambda i, idx=idx, blk=blk: (idx, blk, 0)))
        w_arrays.append(w_arr)
    mem_index = lambda i: (layer, i // tiles_per_seq, 0, 0)
    in_specs += [
        pl.BlockSpec((tm, d_model), row),
        _layer_vec_spec(layer, d_model),
        _layer_vec_spec(layer, d_model),
        _const_spec((None, d_model, d_model), lambda i: (layer, 0, 0)),
        pl.BlockSpec((None, None, d_model, mlen), mem_index),
        pl.BlockSpec((None, None, mlen, d_model), mem_index),
        _const_spec((None, d_model, d_model), lambda i: (layer, 0, 0)),
        _layer_vec_spec(layer, d_model),
        _layer_vec_spec(layer, d_model),
    ]
    kern = functools.partial(_cross_kernel, n_parts=len(y_parts))
    return pl.pallas_call(
        kern,
        grid=(n // tm,),
        in_specs=in_specs,
        out_specs=[pl.BlockSpec((tm, d_model), row), pl.BlockSpec((tm, d_model), row)],
        out_shape=[jax.ShapeDtypeStruct((n, d_model), F32),
                   jax.ShapeDtypeStruct((n, d_model), BF16)],
        compiler_params=_params(("arbitrary",)),
        name="cross",
    )(*y_parts, *w_arrays, x2, g_post, g_cpre, w_xq, k_mem, v_mem, w_xo, g_cpost, g_fpre)


def _ffn(h, x2, layer, w_gu, w_down, g_post):
    n, d_model = x2.shape
    d_ff = w_down.shape[1]
    tm = ROW_TILE
    row = lambda i: (i, 0)
    return pl.pallas_call(
        _ffn_kernel,
        grid=(n // tm,),
        in_specs=[
            pl.BlockSpec((tm, d_model), row),
            pl.BlockSpec((tm, d_model), row),
            _const_spec((None, d_model, d_ff), lambda i: (layer, 0, 0)),
            _const_spec((None, d_model, d_ff), lambda i: (layer, 0, 1)),
            _const_spec((None, d_ff, d_model), lambda i: (layer, 0, 0)),
            _layer_vec_spec(layer, d_model),
        ],
        out_specs=pl.BlockSpec((tm, d_model), row),
        out_shape=jax.ShapeDtypeStruct((n, d_model), F32),
        compiler_params=_params(("arbitrary",)),
        name="ffn",
    )(h, x2, w_gu, w_gu, w_down, g_post)


def _pad_heads(w, n_heads, slot):
    e, d, _ = w.shape
    w = w.reshape(e, d, n_heads, FOX_HEAD_DIM)
    w = jnp.pad(w, ((0, 0), (0, 0), (0, 0), (0, slot - FOX_HEAD_DIM)))
    return w.reshape(e, d, n_heads * slot)


def _bias_placement(n_heads):
    pad_rows = VT_SLOT - FOX_HEAD_DIM
    pqt = [[0.0] * LANES for _ in range(n_heads * pad_rows)]
    pk = [[0.0] * (n_heads * HEAD_SLOT) for _ in range(LANES)]
    one_lane = 24
    for h in range(n_heads):
        base = h * HEAD_SLOT + FOX_HEAD_DIM
        for part in range(3):
            pqt[h * pad_rows + part][part * 8 + h] = 1.0
            pk[one_lane][base + part] = 1.0
            pqt[h * pad_rows + 3 + part][one_lane] = 1.0
            pk[part * 8 + h][base + 3 + part] = -1.0
    return jnp.array(pqt, BF16), jnp.array(pk, BF16)


def kernel(x, mem, g_mix_pre, g_mix_post, g_cross_pre, g_mem, g_cross_post, g_ffn_pre, g_ffn_post, w_xq, w_xkv, w_xo, w_ffn_gu, w_ffn_down, ab_w_in, ab_b_f, ab_conv_w, ab_w_out, c_w_in, c_conv_w, c_conv_b, c_w_a, c_b_a, c_w_i, c_b_i, c_lam, c_w_out):
    bsz, seq, d_model = x.shape
    depth = g_mix_pre.shape[0]
    n = bsz * seq
    n_heads = ab_b_f.shape[1]
    fox_w = n_heads * FOX_HEAD_DIM
    sc_w = ab_conv_w.shape[2]
    assert n_heads <= 8 and seq % ROW_TILE == 0 and seq % LRU_STEPS == 0 and seq % ATT_TQ == 0

    vec3 = lambda g: g.reshape(g.shape[0], 1, g.shape[-1])
    g_mix_pre, g_mix_post, g_cross_pre, g_mem, g_cross_post, g_ffn_pre, g_ffn_post = map(
        vec3, (g_mix_pre, g_mix_post, g_cross_pre, g_mem, g_cross_post, g_ffn_pre, g_ffn_post))

    i1, i2, i3 = fox_w, 2 * fox_w, 3 * fox_w
    i4 = i3 + n_heads
    i5, i6 = i4 + sc_w, i4 + 2 * sc_w
    w_f = ab_w_in[:, :, i3:i4]
    w_k = _pad_heads(ab_w_in[:, :, i1:i2], n_heads, HEAD_SLOT)
    w_k = w_k.at[:, :, F_LANE0:F_LANE0 + 3 * n_heads].set(jnp.concatenate([w_f, w_f, w_f], -1))
    pieces = [w_k, ab_w_in[:, :, i4:i5], ab_w_in[:, :, i5:i6], ab_w_in[:, :, i6:]]
    w_t = jnp.concatenate([ab_w_in[:, :, 0:i1], ab_w_in[:, :, i2:i3]], axis=-1)
    w_t = w_t.transpose(0, 2, 1).astype(BF16)
    offs = []
    for p in pieces:
        offs.append((offs[-1] if offs else 0) + p.shape[-1])
    offs = tuple(offs)
    w_even = jnp.concatenate(pieces, axis=-1).astype(BF16)
    b_f3 = jnp.pad(jnp.concatenate([ab_b_f, ab_b_f, ab_b_f], axis=-1),
                   ((0, 0), (0, LANES - 3 * n_heads))).reshape(-1, 1, LANES)
    tri = (lax.broadcasted_iota(jnp.int32, (ROW_TILE, ROW_TILE), 0)
           >= lax.broadcasted_iota(jnp.int32, (ROW_TILE, ROW_TILE), 1)).astype(BF16)
    pqt, pk = _bias_placement(n_heads)

    w_out_even = ab_w_out.astype(BF16)
    w_c_in = c_w_in.astype(BF16)
    w_c_out = c_w_out.astype(BF16)
    w_a = c_w_a.astype(BF16)
    w_i = c_w_i.astype(BF16)
    b_a = c_b_a.reshape(c_b_a.shape[0], 1, -1)
    b_i = c_b_i.reshape(c_b_i.shape[0], 1, -1)
    conv_b = vec3(c_conv_b)
    lam = vec3(c_lam)
    w_xq_b = w_xq.astype(BF16)
    w_xo_b = w_xo.astype(BF16)
    w_gu_b = w_ffn_gu.astype(BF16)
    w_down_b = w_ffn_down.astype(BF16)

    w_kt = w_xkv[:, :, 0:d_model].transpose(0, 2, 1).astype(BF16)
    k_mem, v_mem = _mem_kv(mem, g_mem, w_kt, w_xkv.astype(BF16))

    x2 = x.reshape(n, d_model)
    for layer in range(depth):
        if layer % 2 == 0:
            e = layer // 2
            qt, ka, vt, yb = _even_in(x2, g_mix_pre, layer, w_even, w_t, b_f3,
                                      ab_conv_w, e, tri, pqt, pk, seq, offs)
            ya = _fox_attn(qt, ka, vt, seq)
            y_parts = [ya, yb]
            assert fox_w == sc_w
            w_parts = [(w_out_even, e, 0), (w_out_even, e, 1)]
        else:
            o = layer // 2
            ym = _lru(x2, g_mix_pre, layer, w_c_in, c_conv_w, conv_b, w_a, b_a, w_i, b_i,
                      lam, o, seq)
            y_parts = [ym]
            w_parts = [(w_c_out, o, 0)]
        x2, h = _cross(y_parts, w_parts, x2, layer, g_mix_post, g_cross_pre, w_xq_b,
                       k_mem, v_mem, w_xo_b, g_cross_post, g_ffn_pre, seq)
        x2 = _ffn(h, x2, layer, w_gu_b, w_down_b, g_ffn_post)
    return x2.reshape(bsz, seq, d_model)
```

```python
import functools
import math

import jax
import jax.numpy as jnp
from jax import lax
from jax.experimental import pallas as pl
from jax.experimental.pallas import tpu as pltpu

F32 = jnp.float32
BF16 = jnp.bfloat16

EPS = 1e-6
NEG_INF = -1e30
FOX_HEAD_DIM = 64
SC_K = 3
LRU_BW = 256
RG_CONV_K = 4
RG_C = 8.0
MEM_HEADS = 4

LANES = 128
SUBLANES = 8
HEAD_SLOT = LANES
VT_SLOT = 80
F_LANE0 = FOX_HEAD_DIM + 6
LOG2_E = math.log2(math.e)
VMEM_LIMIT = 56 * 1024 * 1024

ROW_TILE = 512
CROSS_TILE = 1024
CROSS_SPLIT = 4
LRU_STEPS = 64
LRU_PITCH = LRU_STEPS + SUBLANES
ATT_TQ = 256
ATT_TK = 256
ATT_HEADS_PER_STEP = 8
FFN_CHUNKS = ((0, 1024), (1024, 1024), (2048, 768))

NT_DIMS = (((1,), (1,)), ((), ()))


def _dot(a, b):
    return jnp.dot(a, b, preferred_element_type=F32)


def _dot_nt(a, b):
    return lax.dot_general(a, b, NT_DIMS, preferred_element_type=F32)


def _rms(x, g):
    ms = jnp.mean(x * x, axis=-1, keepdims=True)
    return x * lax.rsqrt(ms + EPS) * g


def _split3(x):
    hi = x.astype(BF16).astype(F32)
    r = x - hi
    mid = r.astype(BF16).astype(F32)
    lo = (r - mid).astype(BF16).astype(F32)
    return hi, mid, lo


def _causal_conv(cur, prev_tail, w_ref, k_width):
    t = cur.shape[0]
    head_cur = cur[0:SUBLANES, :]
    row8 = lax.broadcasted_iota(jnp.int32, head_cur.shape, 0)
    main = cur * w_ref[k_width - 1:k_width, :]
    head = head_cur * w_ref[k_width - 1:k_width, :]
    for d in range(1, k_width):
        wk = w_ref[k_width - 1 - d:k_width - d, :]
        main = main + pltpu.roll(cur, d, 0) * wk
        shifted = jnp.where(row8 < d, pltpu.roll(prev_tail, d, 0), pltpu.roll(head_cur, d, 0))
        head = head + shifted * wk
    del t
    return main, head


def _even_in_kernel(x_ref, g_ref, w_ref, wt_ref, bf_ref, cw_ref, tri_ref,
                    pqt_ref, pk_ref, qt_ref, ka_ref, vt_ref, yb_ref, carry_c, carry_cu, *,
                    tiles_per_seq, offs):
    o_b, o_c, o_u, o_end = offs
    tm = x_ref.shape[0]
    fox_w = wt_ref.shape[0] // 2
    n_heads = fox_w // FOX_HEAD_DIM
    first = (pl.program_id(0) % tiles_per_seq) == 0

    @pl.when(first)
    def _():
        carry_c[...] = jnp.zeros_like(carry_c)
        carry_cu[...] = jnp.zeros_like(carry_cu)

    h = _rms(x_ref[...], g_ref[...]).astype(BF16)

    lane = lax.broadcasted_iota(jnp.int32, (tm, LANES), 1)
    kp = _dot(h, w_ref[:, 0:o_b])
    fl = pltpu.roll(kp[:, 0:LANES], LANES - F_LANE0, 1)
    fl = jnp.where(lane < 24, fl, 0.0) + bf_ref[...]

    lf = jax.nn.log_sigmoid(fl)
    hi, mid, lo = _split3(lf)
    parts = jnp.where(lane < 8, hi, jnp.where(lane < 16, mid, lo)).astype(BF16)
    r = _dot(tri_ref[...], parts)
    cum = r + pltpu.roll(r, LANES - 8, 1) + pltpu.roll(r, LANES - 16, 1)
    cum = jnp.where(lane < 8, cum + carry_c[...], 0.0)
    carry_c[...] = cum[tm - 1:tm, :]
    chi, cmid, clo = _split3(cum * LOG2_E)
    sel = jnp.where(lane < 8, chi,
                    jnp.where(lane < 16, pltpu.roll(cmid, 8, 1),
                              jnp.where(lane < 24, pltpu.roll(clo, 16, 1),
                                        jnp.where(lane == 24, 1.0, 0.0)))).astype(BF16)

    k0 = jnp.where(lane < FOX_HEAD_DIM, kp[:, 0:LANES], 0.0)
    kp = jnp.concatenate([k0, kp[:, LANES:]], axis=-1)
    ka_ref[...] = (kp + _dot(sel, pk_ref[...])).astype(BF16)

    scale = FOX_HEAD_DIM ** -0.5 * LOG2_E
    qt = _dot_nt(wt_ref[0:fox_w, :], h) * scale
    qb = _dot_nt(pqt_ref[...], sel)
    vt = _dot_nt(wt_ref[fox_w:, :], h)
    pad_rows = VT_SLOT - FOX_HEAD_DIM
    q_zero = jnp.zeros((HEAD_SLOT - FOX_HEAD_DIM - pad_rows, tm), F32)
    ones_blk = (lax.broadcasted_iota(jnp.int32, (pad_rows, tm), 0) == 0).astype(F32)
    q_pieces, v_pieces = [], []
    for n in range(n_heads):
        feat = slice(n * FOX_HEAD_DIM, (n + 1) * FOX_HEAD_DIM)
        q_pieces += [qt[feat, :], qb[n * pad_rows:(n + 1) * pad_rows, :], q_zero]
        v_pieces += [vt[feat, :], ones_blk]
    qt = jnp.concatenate(q_pieces, axis=0).astype(BF16)
    vt = jnp.concatenate(v_pieces, axis=0).astype(BF16)
    for c in range(qt_ref.shape[0]):
        w = qt_ref.shape[-1]
        qt_ref[c] = qt[:, c * w:(c + 1) * w]
    for c in range(vt_ref.shape[0]):
        w = vt_ref.shape[-1]
        vt_ref[c] = vt[:, c * w:(c + 1) * w]

    b_gate = _dot(h, w_ref[:, o_b:o_c])
    cu = _dot(h, w_ref[:, o_c:o_u]) * _dot(h, w_ref[:, o_u:o_end])
    main, head = _causal_conv(cu, carry_cu[...], cw_ref, SC_K)
    carry_cu[...] = cu[tm - SUBLANES:tm, :]
    yb_ref[...] = (b_gate * main).astype(BF16)
    yb_ref[0:SUBLANES, :] = (b_gate[0:SUBLANES, :] * head).astype(BF16)


def _fox_attn_kernel(qt_ref, ka_ref, vt_ref, o_ref, acc_ref):
    tq = qt_ref.shape[-1]
    qi = pl.program_id(2)
    krow = lax.broadcasted_iota(jnp.int32, (ATT_TK, tq), 0)
    qcol = lax.broadcasted_iota(jnp.int32, (ATT_TK, tq), 1)
    heads = range(acc_ref.shape[0])

    def step(j, ms, masked):
        k0 = pl.multiple_of(j * ATT_TK, ATT_TK)
        sts = [_dot(ka_ref[pl.ds(k0, ATT_TK), hh * HEAD_SLOT:(hh + 1) * HEAD_SLOT],
                    qt_ref[hh * HEAD_SLOT:(hh + 1) * HEAD_SLOT, :]) for hh in heads]
        ps, alphas, new = [], [], []
        for hh in heads:
            st = sts[hh]
            if masked:
                st = jnp.where(qcol >= krow, st, NEG_INF)
            m_new = jnp.maximum(ms[hh], jnp.max(st, axis=0, keepdims=True))
            alphas.append(jnp.exp2(ms[hh] - m_new))
            ps.append(jnp.exp2(st - m_new).astype(BF16))
            new.append(m_new)
        for hh in heads:
            vt = vt_ref[j, hh * VT_SLOT:(hh + 1) * VT_SLOT, :]
            acc_ref[hh] = alphas[hh] * acc_ref[hh] + _dot(vt, ps[hh])
        return tuple(new)

    acc_ref[...] = jnp.zeros_like(acc_ref)
    ms = tuple(jnp.full((1, tq), NEG_INF, F32) for _ in heads)
    ms = lax.fori_loop(0, qi, functools.partial(step, masked=False), ms)
    step(qi, ms, True)
    o_t = jnp.concatenate(
        [acc_ref[hh, 0:FOX_HEAD_DIM, :] / acc_ref[hh, FOX_HEAD_DIM:FOX_HEAD_DIM + 1, :]
         for hh in heads], axis=0)
    o_ref[...] = o_t.T.astype(BF16)


def _lru_kernel(x_ref, g_ref, win_ref, cw_ref, cb_ref, wa_ref, ba_ref, wi_ref, bi_ref,
                lam_ref, y_ref, carry_h, carry_u, in_scr, out_scr):
    bsz, steps, width = x_ref.shape
    tm = steps * bsz
    tail = (RG_CONV_K - 1) * bsz
    lane_tiles = range(width // LANES)

    @pl.when(pl.program_id(0) == 0)
    def _():
        carry_h[...] = jnp.zeros_like(carry_h)
        carry_u[...] = jnp.zeros_like(carry_u)

    for b in range(bsz):
        hb = _rms(x_ref[b], g_ref[...])
        for k in lane_tiles:
            in_scr[k, b * LRU_PITCH:b * LRU_PITCH + steps, :] = hb[:, k * LANES:(k + 1) * LANES]
    h = jnp.concatenate(
        [jnp.concatenate([in_scr[k, pl.ds(t, bsz, stride=LRU_PITCH), :] for k in lane_tiles],
                         axis=-1) for t in range(steps)], axis=0).astype(BF16)
    u = _dot(h, win_ref[:, width:2 * width])

    ext = jnp.concatenate([carry_u[...], u], axis=0)
    carry_u[...] = u[tm - tail:tm, :]
    uc = cb_ref[...] + u * cw_ref[RG_CONV_K - 1:RG_CONV_K, :]
    for d in range(1, RG_CONV_K):
        shifted = ext[tail - d * bsz:tail - d * bsz + tm, :]
        uc = uc + shifted * cw_ref[RG_CONV_K - 1 - d:RG_CONV_K - d, :]
    ucb = uc.astype(BF16)

    nb = width // LRU_BW
    r = jnp.concatenate(
        [_dot(ucb[:, n * LRU_BW:(n + 1) * LRU_BW], wa_ref[n]) for n in range(nb)], axis=-1)
    r = jax.nn.sigmoid(r + ba_ref[...])
    gi = jnp.concatenate(
        [_dot(ucb[:, n * LRU_BW:(n + 1) * LRU_BW], wi_ref[n]) for n in range(nb)], axis=-1)
    gi = jax.nn.sigmoid(gi + bi_ref[...])

    log_a = r * (-RG_C * jax.nn.softplus(-lam_ref[...]))
    a = jnp.exp(log_a)
    b = jnp.sqrt(-jnp.tanh(log_a) * (1.0 + a * a)) * (gi * uc)

    state = carry_h[...]
    hs = []
    for t in range(steps):
        state = a[t * bsz:(t + 1) * bsz, :] * state + b[t * bsz:(t + 1) * bsz, :]
        hs.append(state)
    carry_h[...] = state
    hs = jnp.concatenate(hs, axis=0)

    y = jax.nn.gelu(_dot(h, win_ref[:, 0:width])) * hs
    for t in range(steps):
        for k in lane_tiles:
            out_scr[k, pl.ds(t, bsz, stride=LRU_PITCH), :] = (
                y[t * bsz:(t + 1) * bsz, k * LANES:(k + 1) * LANES])
    for b in range(bsz):
        y_ref[b] = jnp.concatenate(
            [out_scr[k, b * LRU_PITCH:b * LRU_PITCH + steps, :] for k in lane_tiles],
            axis=-1).astype(BF16)


def _mem_kv_kernel(mem_ref, g_ref, wkt_ref, wv_ref, kt_ref, v_ref):
    m = _rms(mem_ref[...], g_ref[...]).astype(BF16)
    kt_ref[...] = _dot_nt(wkt_ref[...], m).astype(BF16)
    v_ref[...] = _dot(m, wv_ref[...]).astype(BF16)


def _cross_kernel(*refs, n_parts):
    y_refs = refs[0:n_parts]
    w_refs = refs[n_parts:2 * n_parts]
    (x_ref, gpost_ref, gcpre_ref, wq_ref, kt_ref, v_ref, wo_ref, gcpost_ref, gfpre_ref,
     xo_ref, ho_ref) = refs[2 * n_parts:]
    tm, d_model = x_ref.shape
    hd = d_model // MEM_HEADS
    sub = tm // CROSS_SPLIT
    rows = [pl.ds(s * sub, sub) for s in range(CROSS_SPLIT)]
    each = lambda f, *lists: [f(*args) for args in zip(*lists)]

    def mix_out(r):
        y = _dot(y_refs[0][r, :], w_refs[0][...])
        for p in range(1, n_parts):
            y = y + _dot(y_refs[p][r, :], w_refs[p][...])
        return y

    ys = each(mix_out, rows)
    x1s = each(lambda r, y: x_ref[r, :] + _rms(y, gpost_ref[...]), rows, ys)
    hs = each(lambda x1: _rms(x1, gcpre_ref[...]).astype(BF16), x1s)
    qs = each(lambda h: (_dot(h, wq_ref[...]) * (hd ** -0.5)).astype(BF16), hs)
    outs = [[] for _ in rows]
    for n in range(MEM_HEADS):
        ss = each(lambda q: _dot(q[:, n * hd:(n + 1) * hd], kt_ref[n * hd:(n + 1) * hd, :]), qs)
        es = each(lambda s: jnp.exp(s - jnp.max(s, axis=-1, keepdims=True)), ss)
        os_ = each(lambda e: _dot(e.astype(BF16), v_ref[:, n * hd:(n + 1) * hd])
                   / jnp.sum(e, axis=-1, keepdims=True), es)
        for lst, o in zip(outs, os_):
            lst.append(o.astype(BF16))
    y2s = each(lambda o: _dot(jnp.concatenate(o, axis=-1), wo_ref[...]), outs)
    x2s = each(lambda x1, y2: x1 + _rms(y2, gcpost_ref[...]), x1s, y2s)
    for r, x2 in zip(rows, x2s):
        xo_ref[r, :] = x2
        ho_ref[r, :] = _rms(x2, gfpre_ref[...]).astype(BF16)


def _ffn_kernel(h_ref, x_ref, wg_ref, wu_ref, wd_ref, g_ref, xo_ref):
    h = h_ref[...]
    acc = None
    for c0, cw in FFN_CHUNKS:
        a = (jax.nn.silu(_dot(h, wg_ref[:, c0:c0 + cw])) * _dot(h, wu_ref[:, c0:c0 + cw]))
        part = _dot(a.astype(BF16), wd_ref[c0:c0 + cw, :])
        acc = part if acc is None else acc + part
    xo_ref[...] = x_ref[...] + _rms(acc, g_ref[...])


def _params(sem):
    return pltpu.CompilerParams(dimension_semantics=sem, vmem_limit_bytes=VMEM_LIMIT)


def _const_spec(shape, index):
    return pl.BlockSpec(shape, index, pipeline_mode=pl.Buffered(1))


def _layer_vec_spec(layer, width):
    return _const_spec((None, 1, width), lambda *_: (layer, 0, 0))


def _even_in(x2, g_pre, layer, w_all, w_t, bf_pad, conv_w, e, tri, pqt, pk, seq, offs):
    n, d_model = x2.shape
    tm = ROW_TILE
    sc_w = offs[2] - offs[1]
    q_rows = pk.shape[1]
    n_heads = q_rows // HEAD_SLOT
    vt_rows = n_heads * VT_SLOT
    tiles_per_seq = seq // tm
    row = lambda i: (i, 0)
    chunked = lambda i: (i // tiles_per_seq, i % tiles_per_seq, 0, 0)
    kern = functools.partial(_even_in_kernel, tiles_per_seq=tiles_per_seq, offs=offs)
    return pl.pallas_call(
        kern,
        grid=(n // tm,),
        in_specs=[
            pl.BlockSpec((tm, d_model), row),
            _layer_vec_spec(layer, d_model),
            _const_spec((None, d_model, offs[-1]), lambda i: (e, 0, 0)),
            _const_spec((None, w_t.shape[1], d_model), lambda i: (e, 0, 0)),
            _const_spec((None, 1, LANES), lambda i: (e, 0, 0)),
            _const_spec((None, SC_K, sc_w), lambda i: (e, 0, 0)),
            _const_spec((tm, tm), lambda i: (0, 0)),
            _const_spec(pqt.shape, lambda i: (0, 0)),
            _const_spec((LANES, q_rows), lambda i: (0, 0)),
        ],
        out_specs=[
            pl.BlockSpec((None, tm // ATT_TQ, q_rows, ATT_TQ), chunked),
            pl.BlockSpec((tm, q_rows), row),
            pl.BlockSpec((None, tm // ATT_TK, vt_rows, ATT_TK), chunked),
            pl.BlockSpec((tm, sc_w), row),
        ],
        out_shape=[
            jax.ShapeDtypeStruct((n // seq, seq // ATT_TQ, q_rows, ATT_TQ), BF16),
            jax.ShapeDtypeStruct((n, q_rows), BF16),
            jax.ShapeDtypeStruct((n // seq, seq // ATT_TK, vt_rows, ATT_TK), BF16),
            jax.ShapeDtypeStruct((n, sc_w), BF16),
        ],
        scratch_shapes=[pltpu.VMEM((1, LANES), F32), pltpu.VMEM((SUBLANES, sc_w), F32)],
        compiler_params=_params(("arbitrary",)),
        name="even_in",
    )(x2, g_pre, w_all, w_t, bf_pad, conv_w, tri, pqt, pk)


def _fox_attn(qt, ka, vt, seq):
    bsz, nq, _, _ = qt.shape
    _, nkb, vt_rows, _ = vt.shape
    hps = ATT_HEADS_PER_STEP
    n_groups = vt_rows // (hps * VT_SLOT)
    return pl.pallas_call(
        _fox_attn_kernel,
        grid=(bsz, n_groups, nq),
        in_specs=[
            pl.BlockSpec((None, None, hps * HEAD_SLOT, ATT_TQ), lambda b, p, i: (b, i, p, 0)),
            pl.BlockSpec((seq, hps * HEAD_SLOT), lambda b, p, i: (b, p)),
            pl.BlockSpec((None, nkb, hps * VT_SLOT, ATT_TK), lambda b, p, i: (b, 0, p, 0)),
        ],
        out_specs=pl.BlockSpec((ATT_TQ, hps * FOX_HEAD_DIM), lambda b, p, i: (b * nq + i, p)),
        out_shape=jax.ShapeDtypeStruct((bsz * seq, n_groups * hps * FOX_HEAD_DIM), BF16),
        scratch_shapes=[pltpu.VMEM((hps, VT_SLOT, ATT_TQ), F32)],
        compiler_params=_params(("arbitrary", "arbitrary", "arbitrary")),
        name="fox_attn",
    )(qt, ka, vt)


def _lru(x2, g_pre, layer, w_in, conv_w, conv_b, w_a, b_a, w_i, b_i, lam, o, seq):
    n, d_model = x2.shape
    bsz = n // seq
    assert bsz == SUBLANES
    steps = LRU_STEPS
    nb = d_model // LRU_BW
    vec = lambda: _const_spec((None, 1, d_model), lambda i: (o, 0, 0))
    blk = pl.BlockSpec((bsz, steps, d_model), lambda i: (0, i, 0))
    relayout = pltpu.VMEM((d_model // LANES, bsz * LRU_PITCH, LANES), F32)
    y = pl.pallas_call(
        _lru_kernel,
        grid=(seq // steps,),
        in_specs=[
            blk,
            _layer_vec_spec(layer, d_model),
            _const_spec((None, d_model, 2 * d_model), lambda i: (o, 0, 0)),
            _const_spec((None, RG_CONV_K, d_model), lambda i: (o, 0, 0)),
            vec(),
            _const_spec((None, nb, LRU_BW, LRU_BW), lambda i: (o, 0, 0, 0)),
            vec(),
            _const_spec((None, nb, LRU_BW, LRU_BW), lambda i: (o, 0, 0, 0)),
            vec(),
            vec(),
        ],
        out_specs=blk,
        out_shape=jax.ShapeDtypeStruct((bsz, seq, d_model), BF16),
        scratch_shapes=[pltpu.VMEM((bsz, d_model), F32),
                        pltpu.VMEM(((RG_CONV_K - 1) * bsz, d_model), F32),
                        relayout, relayout],
        compiler_params=_params(("arbitrary",)),
        name="rglru",
    )(x2.reshape(bsz, seq, d_model), g_pre, w_in, conv_w, conv_b, w_a, b_a, w_i, b_i, lam)
    return y.reshape(n, d_model)


def _mem_kv(mem, g_mem, w_kt, w_xkv):
    bsz, mlen, d_model = mem.shape
    depth = w_xkv.shape[0]
    return pl.pallas_call(
        _mem_kv_kernel,
        grid=(depth, bsz),
        in_specs=[
            pl.BlockSpec((None, mlen, d_model), lambda l, b: (b, 0, 0)),
            pl.BlockSpec((None, 1, d_model), lambda l, b: (l, 0, 0)),
            pl.BlockSpec((None, d_model, d_model), lambda l, b: (l, 0, 0)),
            pl.BlockSpec((None, d_model, d_model), lambda l, b: (l, 0, 1)),
        ],
        out_specs=[
            pl.BlockSpec((None, None, d_model, mlen), lambda l, b: (l, b, 0, 0)),
            pl.BlockSpec((None, None, mlen, d_model), lambda l, b: (l, b, 0, 0)),
        ],
        out_shape=[jax.ShapeDtypeStruct((depth, bsz, d_model, mlen), BF16),
                   jax.ShapeDtypeStruct((depth, bsz, mlen, d_model), BF16)],
        compiler_params=_params(("arbitrary", "arbitrary")),
        name="mem_kv",
    )(mem, g_mem, w_kt, w_xkv)


def _cross(y_parts, w_parts, x2, layer, g_post, g_cpre, w_xq, k_mem, v_mem, w_xo,
           g_cpost, g_fpre, seq):
    n, d_model = x2.shape
    tm = CROSS_TILE
    tiles_per_seq = seq // tm
    mlen = v_mem.shape[2]
    row = lambda i: (i, 0)
    in_specs = [pl.BlockSpec((tm, yp.shape[1]), row) for yp in y_parts]
    w_arrays = []
    for yp, (w_arr, idx, blk) in zip(y_parts, w_parts):
        in_specs.append(_const_spec((None, yp.shape[1], d_model),
                                    lambda i, idx=idx, blk=blk: (idx, blk, 0)))
        w_arrays.append(w_arr)
    mem_index = lambda i: (layer, i // tiles_per_seq, 0, 0)
    in_specs += [
        pl.BlockSpec((tm, d_model), row),
        _layer_vec_spec(layer, d_model),
        _layer_vec_spec(layer, d_model),
        _const_spec((None, d_model, d_model), lambda i: (layer, 0, 0)),
        pl.BlockSpec((None, None, d_model, mlen), mem_index),
        pl.BlockSpec((None, None, mlen, d_model), mem_index),
        _const_spec((None, d_model, d_model), lambda i: (layer, 0, 0)),
        _layer_vec_spec(layer, d_model),
        _layer_vec_spec(layer, d_model),
    ]
    kern = functools.partial(_cross_kernel, n_parts=len(y_parts))
    return pl.pallas_call(
        kern,
        grid=(n // tm,),
        in_specs=in_specs,
        out_specs=[pl.BlockSpec((tm, d_model), row), pl.BlockSpec((tm, d_model), row)],
        out_shape=[jax.ShapeDtypeStruct((n, d_model), F32),
                   jax.ShapeDtypeStruct((n, d_model), BF16)],
        compiler_params=_params(("arbitrary",)),
        name="cross",
    )(*y_parts, *w_arrays, x2, g_post, g_cpre, w_xq, k_mem, v_mem, w_xo, g_cpost, g_fpre)


def _ffn(h, x2, layer, w_gu, w_down, g_post):
    n, d_model = x2.shape
    d_ff = w_down.shape[1]
    tm = ROW_TILE
    row = lambda i: (i, 0)
    return pl.pallas_call(
        _ffn_kernel,
        grid=(n // tm,),
        in_specs=[
            pl.BlockSpec((tm, d_model), row),
            pl.BlockSpec((tm, d_model), row),
            _const_spec((None, d_model, d_ff), lambda i: (layer, 0, 0)),
            _const_spec((None, d_model, d_ff), lambda i: (layer, 0, 1)),
            _const_spec((None, d_ff, d_model), lambda i: (layer, 0, 0)),
            _layer_vec_spec(layer, d_model),
        ],
        out_specs=pl.BlockSpec((tm, d_model), row),
        out_shape=jax.ShapeDtypeStruct((n, d_model), F32),
        compiler_params=_params(("arbitrary",)),
        name="ffn",
    )(h, x2, w_gu, w_gu, w_down, g_post)


def _pad_heads(w, n_heads, slot):
    e, d, _ = w.shape
    w = w.reshape(e, d, n_heads, FOX_HEAD_DIM)
    w = jnp.pad(w, ((0, 0), (0, 0), (0, 0), (0, slot - FOX_HEAD_DIM)))
    return w.reshape(e, d, n_heads * slot)


def _bias_placement(n_heads):
    pad_rows = VT_SLOT - FOX_HEAD_DIM
    pqt = [[0.0] * LANES for _ in range(n_heads * pad_rows)]
    pk = [[0.0] * (n_heads * HEAD_SLOT) for _ in range(LANES)]
    one_lane = 24
    for h in range(n_heads):
        base = h * HEAD_SLOT + FOX_HEAD_DIM
        for part in range(3):
            pqt[h * pad_rows + part][part * 8 + h] = 1.0
            pk[one_lane][base + part] = 1.0
            pqt[h * pad_rows + 3 + part][one_lane] = 1.0
            pk[part * 8 + h][base + 3 + part] = -1.0
    return jnp.array(pqt, BF16), jnp.array(pk, BF16)


def kernel(x, mem, g_mix_pre, g_mix_post, g_cross_pre, g_mem, g_cross_post, g_ffn_pre, g_ffn_post, w_xq, w_xkv, w_xo, w_ffn_gu, w_ffn_down, ab_w_in, ab_b_f, ab_conv_w, ab_w_out, c_w_in, c_conv_w, c_conv_b, c_w_a, c_b_a, c_w_i, c_b_i, c_lam, c_w_out):
    bsz, seq, d_model = x.shape
    depth = g_mix_pre.shape[0]
    n = bsz * seq
    n_heads = ab_b_f.shape[1]
    fox_w = n_heads * FOX_HEAD_DIM
    sc_w = ab_conv_w.shape[2]
    assert n_heads <= 8 and seq % ROW_TILE == 0 and seq % LRU_STEPS == 0 and seq % ATT_TQ == 0

    vec3 = lambda g: g.reshape(g.shape[0], 1, g.shape[-1])
    g_mix_pre, g_mix_post, g_cross_pre, g_mem, g_cross_post, g_ffn_pre, g_ffn_post = map(
        vec3, (g_mix_pre, g_mix_post, g_cross_pre, g_mem, g_cross_post, g_ffn_pre, g_ffn_post))

    i1, i2, i3 = fox_w, 2 * fox_w, 3 * fox_w
    i4 = i3 + n_heads
    i5, i6 = i4 + sc_w, i4 + 2 * sc_w
    w_f = ab_w_in[:, :, i3:i4]
    w_k = _pad_heads(ab_w_in[:, :, i1:i2], n_heads, HEAD_SLOT)
    w_k = w_k.at[:, :, F_LANE0:F_LANE0 + 3 * n_heads].set(jnp.concatenate([w_f, w_f, w_f], -1))
    pieces = [w_k, ab_w_in[:, :, i4:i5], ab_w_in[:, :, i5:i6], ab_w_in[:, :, i6:]]
    w_t = jnp.concatenate([ab_w_in[:, :, 0:i1], ab_w_in[:, :, i2:i3]], axis=-1)
    w_t = w_t.transpose(0, 2, 1).astype(BF16)
    offs = []
    for p in pieces:
        offs.append((offs[-1] if offs else 0) + p.shape[-1])
    offs = tuple(offs)
    w_even = jnp.concatenate(pieces, axis=-1).astype(BF16)
    b_f3 = jnp.pad(jnp.concatenate([ab_b_f, ab_b_f, ab_b_f], axis=-1),
                   ((0, 0), (0, LANES - 3 * n_heads))).reshape(-1, 1, LANES)
    tri = (lax.broadcasted_iota(jnp.int32, (ROW_TILE, ROW_TILE), 0)
           >= lax.broadcasted_iota(jnp.int32, (ROW_TILE, ROW_TILE), 1)).astype(BF16)
    pqt, pk = _bias_placement(n_heads)

    w_out_even = ab_w_out.astype(BF16)
    w_c_in = c_w_in.astype(BF16)
    w_c_out = c_w_out.astype(BF16)
    w_a = c_w_a.astype(BF16)
    w_i = c_w_i.astype(BF16)
    b_a = c_b_a.reshape(c_b_a.shape[0], 1, -1)
    b_i = c_b_i.reshape(c_b_i.shape[0], 1, -1)
    conv_b = vec3(c_conv_b)
    lam = vec3(c_lam)
    w_xq_b = w_xq.astype(BF16)
    w_xo_b = w_xo.astype(BF16)
    w_gu_b = w_ffn_gu.astype(BF16)
    w_down_b = w_ffn_down.astype(BF16)

    w_kt = w_xkv[:, :, 0:d_model].transpose(0, 2, 1).astype(BF16)
    k_mem, v_mem = _mem_kv(mem, g_mem, w_kt, w_xkv.astype(BF16))

    x2 = x.reshape(n, d_model)
    for layer in range(depth):
        if layer % 2 == 0:
            e = layer // 2
            qt, ka, vt, yb = _even_in(x2, g_mix_pre, layer, w_even, w_t, b_f3,
                                      ab_conv_w, e, tri, pqt, pk, seq, offs)
            ya = _fox_attn(qt, ka, vt, seq)
            y_parts = [ya, yb]
            assert fox_w == sc_w
            w_parts = [(w_out_even, e, 0), (w_out_even, e, 1)]
        else:
            o = layer // 2
            ym = _lru(x2, g_mix_pre, layer, w_c_in, c_conv_w, conv_b, w_a, b_a, w_i, b_i,
                      lam, o, seq)
            y_parts = [ym]
            w_parts = [(w_c_out, o, 0)]
        x2, h = _cross(y_parts, w_parts, x2, layer, g_mix_post, g_cross_pre, w_xq_b,
                       k_mem, v_mem, w_xo_b, g_cross_post, g_ffn_pre, seq)
        x2 = _ffn(h, x2, layer, w_gu_b, w_down_b, g_ffn_post)
    return x2.reshape(bsz, seq, d_model)
```

```python
import functools
import math

import jax
import jax.numpy as jnp
from jax import lax
from jax.experimental import pallas as pl
from jax.experimental.pallas import tpu as pltpu

F32 = jnp.float32
BF16 = jnp.bfloat16

EPS = 1e-6
NEG_INF = -1e30
FOX_HEAD_DIM = 64
SC_K = 3
LRU_BW = 256
RG_CONV_K = 4
RG_C = 8.0
MEM_HEADS = 4

LANES = 128
SUBLANES = 8
HEAD_SLOT = LANES
VT_SLOT = 80
F_LANE0 = FOX_HEAD_DIM + 6
LOG2_E = math.log2(math.e)
VMEM_LIMIT = 56 * 1024 * 1024

ROW_TILE = 512
CROSS_TILE = 1024
CROSS_SPLIT = 4
MEM_KV_SPLIT = 2
LRU_STEPS = 64
LRU_PITCH = LRU_STEPS + SUBLANES
ATT_TQ = 256
ATT_TK = 256
ATT_HEADS_PER_STEP = 8
FFN_CHUNKS = ((0, 1024), (1024, 1024), (2048, 768))

NT_DIMS = (((1,), (1,)), ((), ()))


def _dot(a, b):
    return jnp.dot(a, b, preferred_element_type=F32)


def _dot_nt(a, b):
    return lax.dot_general(a, b, NT_DIMS, preferred_element_type=F32)


def _rms(x, g):
    ms = jnp.mean(x * x, axis=-1, keepdims=True)
    return x * lax.rsqrt(ms + EPS) * g


def _split3(x):
    hi = x.astype(BF16).astype(F32)
    r = x - hi
    mid = r.astype(BF16).astype(F32)
    lo = (r - mid).astype(BF16).astype(F32)
    return hi, mid, lo


def _causal_conv(cur, prev_tail, w_ref, k_width):
    head_cur = cur[0:SUBLANES, :]
    row8 = lax.broadcasted_iota(jnp.int32, head_cur.shape, 0)
    main = cur * w_ref[k_width - 1:k_width, :]
    head = head_cur * w_ref[k_width - 1:k_width, :]
    for d in range(1, k_width):
        wk = w_ref[k_width - 1 - d:k_width - d, :]
        main = main + pltpu.roll(cur, d, 0) * wk
        shifted = jnp.where(row8 < d, pltpu.roll(prev_tail, d, 0), pltpu.roll(head_cur, d, 0))
        head = head + shifted * wk
    return main, head


def _even_in_kernel(x_ref, g_ref, w_ref, wt_ref, bf_ref, cw_ref, tri_ref,
                    pqt_ref, pk_ref, qt_ref, ka_ref, vt_ref, yb_ref, carry_c, carry_cu, *,
                    tiles_per_seq, offs):
    o_b, o_c, o_u, o_end = offs
    tm = x_ref.shape[0]
    fox_w = wt_ref.shape[0] // 2
    n_heads = fox_w // FOX_HEAD_DIM
    first = (pl.program_id(0) % tiles_per_seq) == 0

    @pl.when(first)
    def _():
        carry_c[...] = jnp.zeros_like(carry_c)
        carry_cu[...] = jnp.zeros_like(carry_cu)

    h = _rms(x_ref[...], g_ref[...]).astype(BF16)

    lane = lax.broadcasted_iota(jnp.int32, (tm, LANES), 1)
    kp = _dot(h, w_ref[:, 0:o_b])
    fl = pltpu.roll(kp[:, 0:LANES], LANES - F_LANE0, 1)
    fl = jnp.where(lane < 24, fl, 0.0) + bf_ref[...]

    lf = jax.nn.log_sigmoid(fl)
    hi, mid, lo = _split3(lf)
    parts = jnp.where(lane < 8, hi, jnp.where(lane < 16, mid, lo)).astype(BF16)
    r = _dot(tri_ref[...], parts)
    cum = r + pltpu.roll(r, LANES - 8, 1) + pltpu.roll(r, LANES - 16, 1)
    cum = jnp.where(lane < 8, cum + carry_c[...], 0.0)
    carry_c[...] = cum[tm - 1:tm, :]
    chi, cmid, clo = _split3(cum * LOG2_E)
    sel = jnp.where(lane < 8, chi,
                    jnp.where(lane < 16, pltpu.roll(cmid, 8, 1),
                              jnp.where(lane < 24, pltpu.roll(clo, 16, 1),
                                        jnp.where(lane == 24, 1.0, 0.0)))).astype(BF16)

    k0 = jnp.where(lane < FOX_HEAD_DIM, kp[:, 0:LANES], 0.0)
    kp = jnp.concatenate([k0, kp[:, LANES:]], axis=-1)
    ka_ref[...] = (kp + _dot(sel, pk_ref[...])).astype(BF16)

    scale = FOX_HEAD_DIM ** -0.5 * LOG2_E
    qt = _dot_nt(wt_ref[0:fox_w, :], h) * scale
    qb = _dot_nt(pqt_ref[...], sel)
    vt = _dot_nt(wt_ref[fox_w:, :], h)
    pad_rows = VT_SLOT - FOX_HEAD_DIM
    q_zero = jnp.zeros((HEAD_SLOT - FOX_HEAD_DIM - pad_rows, tm), F32)
    ones_blk = (lax.broadcasted_iota(jnp.int32, (pad_rows, tm), 0) == 0).astype(F32)
    q_pieces, v_pieces = [], []
    for n in range(n_heads):
        feat = slice(n * FOX_HEAD_DIM, (n + 1) * FOX_HEAD_DIM)
        q_pieces += [qt[feat, :], qb[n * pad_rows:(n + 1) * pad_rows, :], q_zero]
        v_pieces += [vt[feat, :], ones_blk]
    qt = jnp.concatenate(q_pieces, axis=0).astype(BF16)
    vt = jnp.concatenate(v_pieces, axis=0).astype(BF16)
    for c in range(qt_ref.shape[0]):
        w = qt_ref.shape[-1]
        qt_ref[c] = qt[:, c * w:(c + 1) * w]
    for c in range(vt_ref.shape[0]):
        w = vt_ref.shape[-1]
        vt_ref[c] = vt[:, c * w:(c + 1) * w]

    b_gate = _dot(h, w_ref[:, o_b:o_c])
    cu = _dot(h, w_ref[:, o_c:o_u]) * _dot(h, w_ref[:, o_u:o_end])
    main, head = _causal_conv(cu, carry_cu[...], cw_ref, SC_K)
    carry_cu[...] = cu[tm - SUBLANES:tm, :]
    yb_ref[...] = (b_gate * main).astype(BF16)
    yb_ref[0:SUBLANES, :] = (b_gate[0:SUBLANES, :] * head).astype(BF16)


def _fox_attn_kernel(qt_ref, ka_ref, vt_ref, o_ref, acc_ref):
    tq = qt_ref.shape[-1]
    qi = pl.program_id(2)
    krow = lax.broadcasted_iota(jnp.int32, (ATT_TK, tq), 0)
    qcol = lax.broadcasted_iota(jnp.int32, (ATT_TK, tq), 1)
    heads = range(acc_ref.shape[0])

    def step(j, ms, masked):
        k0 = pl.multiple_of(j * ATT_TK, ATT_TK)
        sts = [_dot(ka_ref[pl.ds(k0, ATT_TK), hh * HEAD_SLOT:(hh + 1) * HEAD_SLOT],
                    qt_ref[hh * HEAD_SLOT:(hh + 1) * HEAD_SLOT, :]) for hh in heads]
        ps, alphas, new = [], [], []
        for hh in heads:
            st = sts[hh]
            if masked:
                st = jnp.where(qcol >= krow, st, NEG_INF)
            m_new = jnp.maximum(ms[hh], jnp.max(st, axis=0, keepdims=True))
            alphas.append(jnp.exp2(ms[hh] - m_new))
            ps.append(jnp.exp2(st - m_new).astype(BF16))
            new.append(m_new)
        for hh in heads:
            vt = vt_ref[j, hh * VT_SLOT:(hh + 1) * VT_SLOT, :]
            acc_ref[hh] = alphas[hh] * acc_ref[hh] + _dot(vt, ps[hh])
        return tuple(new)

    acc_ref[...] = jnp.zeros_like(acc_ref)
    ms = tuple(jnp.full((1, tq), NEG_INF, F32) for _ in heads)
    ms = lax.fori_loop(0, qi, functools.partial(step, masked=False), ms)
    step(qi, ms, True)
    o_t = jnp.concatenate(
        [acc_ref[hh, 0:FOX_HEAD_DIM, :] / acc_ref[hh, FOX_HEAD_DIM:FOX_HEAD_DIM + 1, :]
         for hh in heads], axis=0)
    o_ref[...] = o_t.T.astype(BF16)


def _lru_kernel(x_ref, g_ref, win_ref, cw_ref, cb_ref, wa_ref, ba_ref, wi_ref, bi_ref,
                lam_ref, y_ref, carry_h, carry_u, in_scr, out_scr):
    bsz, steps, width = x_ref.shape
    tm = steps * bsz
    tail = (RG_CONV_K - 1) * bsz
    lane_tiles = range(width // LANES)

    @pl.when(pl.program_id(0) == 0)
    def _():
        carry_h[...] = jnp.zeros_like(carry_h)
        carry_u[...] = jnp.zeros_like(carry_u)

    for b in range(bsz):
        hb = _rms(x_ref[b], g_ref[...])
        for k in lane_tiles:
            in_scr[k, b * LRU_PITCH:b * LRU_PITCH + steps, :] = hb[:, k * LANES:(k + 1) * LANES]
    h = jnp.concatenate(
        [jnp.concatenate([in_scr[k, pl.ds(t, bsz, stride=LRU_PITCH), :] for k in lane_tiles],
                         axis=-1) for t in range(steps)], axis=0).astype(BF16)
    u = _dot(h, win_ref[:, width:2 * width])

    ext = jnp.concatenate([carry_u[...], u], axis=0)
    carry_u[...] = u[tm - tail:tm, :]
    uc = cb_ref[...] + u * cw_ref[RG_CONV_K - 1:RG_CONV_K, :]
    for d in range(1, RG_CONV_K):
        shifted = ext[tail - d * bsz:tail - d * bsz + tm, :]
        uc = uc + shifted * cw_ref[RG_CONV_K - 1 - d:RG_CONV_K - d, :]
    ucb = uc.astype(BF16)

    nb = width // LRU_BW
    r = jnp.concatenate(
        [_dot(ucb[:, n * LRU_BW:(n + 1) * LRU_BW], wa_ref[n]) for n in range(nb)], axis=-1)
    r = jax.nn.sigmoid(r + ba_ref[...])
    gi = jnp.concatenate(
        [_dot(ucb[:, n * LRU_BW:(n + 1) * LRU_BW], wi_ref[n]) for n in range(nb)], axis=-1)
    gi = jax.nn.sigmoid(gi + bi_ref[...])

    log_a = r * (-RG_C * jax.nn.softplus(-lam_ref[...]))
    a = jnp.exp(log_a)
    b = jnp.sqrt(-jnp.tanh(log_a) * (1.0 + a * a)) * (gi * uc)

    state = carry_h[...]
    hs = []
    for t in range(steps):
        state = a[t * bsz:(t + 1) * bsz, :] * state + b[t * bsz:(t + 1) * bsz, :]
        hs.append(state)
    carry_h[...] = state
    hs = jnp.concatenate(hs, axis=0)

    y = jax.nn.gelu(_dot(h, win_ref[:, 0:width])) * hs
    for t in range(steps):
        for k in lane_tiles:
            out_scr[k, pl.ds(t, bsz, stride=LRU_PITCH), :] = (
                y[t * bsz:(t + 1) * bsz, k * LANES:(k + 1) * LANES])
    for b in range(bsz):
        y_ref[b] = jnp.concatenate(
            [out_scr[k, b * LRU_PITCH:b * LRU_PITCH + steps, :] for k in lane_tiles],
            axis=-1).astype(BF16)


def _mem_kv_kernel(mem_ref, g_ref, w_ref, kt_ref, v_ref):
    nb, mlen, d_model = mem_ref.shape
    m = _rms(mem_ref[...].reshape(nb * mlen, d_model), g_ref[...]).astype(BF16)
    k = _dot(m, w_ref[:, 0:d_model])
    v = _dot(m, w_ref[:, d_model:2 * d_model]).astype(BF16)
    for b in range(nb):
        kt_ref[b] = k[b * mlen:(b + 1) * mlen, :].T.astype(BF16)
        v_ref[b] = v[b * mlen:(b + 1) * mlen, :]


def _cross_kernel(*refs, n_parts):
    y_refs = refs[0:n_parts]
    w_refs = refs[n_parts:2 * n_parts]
    (x_ref, gpost_ref, gcpre_ref, wq_ref, kt_ref, v_ref, wo_ref, gcpost_ref, gfpre_ref,
     xo_ref, ho_ref) = refs[2 * n_parts:]
    tm, d_model = x_ref.shape
    hd = d_model // MEM_HEADS
    sub = tm // CROSS_SPLIT
    rows = [pl.ds(s * sub, sub) for s in range(CROSS_SPLIT)]
    each = lambda f, *lists: [f(*args) for args in zip(*lists)]

    def mix_out(r):
        y = _dot(y_refs[0][r, :], w_refs[0][...])
        for p in range(1, n_parts):
            y = y + _dot(y_refs[p][r, :], w_refs[p][...])
        return y

    ys = each(mix_out, rows)
    x1s = each(lambda r, y: x_ref[r, :] + _rms(y, gpost_ref[...]), rows, ys)
    hs = each(lambda x1: _rms(x1, gcpre_ref[...]).astype(BF16), x1s)
    qs = each(lambda h: (_dot(h, wq_ref[...]) * (hd ** -0.5)).astype(BF16), hs)
    outs = [[] for _ in rows]
    for n in range(MEM_HEADS):
        ss = each(lambda q: _dot(q[:, n * hd:(n + 1) * hd], kt_ref[n * hd:(n + 1) * hd, :]), qs)
        es = each(lambda s: jnp.exp(s - jnp.max(s, axis=-1, keepdims=True)), ss)
        os_ = each(lambda e: _dot(e.astype(BF16), v_ref[:, n * hd:(n + 1) * hd])
                   / jnp.sum(e, axis=-1, keepdims=True), es)
        for lst, o in zip(outs, os_):
            lst.append(o.astype(BF16))
    y2s = each(lambda o: _dot(jnp.concatenate(o, axis=-1), wo_ref[...]), outs)
    x2s = each(lambda x1, y2: x1 + _rms(y2, gcpost_ref[...]), x1s, y2s)
    for r, x2 in zip(rows, x2s):
        xo_ref[r, :] = x2
        ho_ref[r, :] = _rms(x2, gfpre_ref[...]).astype(BF16)


def _ffn_kernel(h_ref, x_ref, wg_ref, wu_ref, wd_ref, g_ref, xo_ref):
    h = h_ref[...]
    acc = None
    for c0, cw in FFN_CHUNKS:
        a = (jax.nn.silu(_dot(h, wg_ref[:, c0:c0 + cw])) * _dot(h, wu_ref[:, c0:c0 + cw]))
        part = _dot(a.astype(BF16), wd_ref[c0:c0 + cw, :])
        acc = part if acc is None else acc + part
    xo_ref[...] = x_ref[...] + _rms(acc, g_ref[...])


def _params(sem):
    return pltpu.CompilerParams(dimension_semantics=sem, vmem_limit_bytes=VMEM_LIMIT)


def _const_spec(shape, index):
    return pl.BlockSpec(shape, index, pipeline_mode=pl.Buffered(1))


def _layer_vec_spec(row, width):
    return _const_spec((None, 1, width), lambda *_: (row, 0, 0))


def _even_in(x2, vecs, r_pre, w_all, w_t, bf_pad, conv_w, e, tri, pqt, pk, seq, offs):
    n, d_model = x2.shape
    tm = ROW_TILE
    sc_w = offs[2] - offs[1]
    q_rows = pk.shape[1]
    n_heads = q_rows // HEAD_SLOT
    vt_rows = n_heads * VT_SLOT
    tiles_per_seq = seq // tm
    row = lambda i: (i, 0)
    chunked = lambda i: (i // tiles_per_seq, i % tiles_per_seq, 0, 0)
    kern = functools.partial(_even_in_kernel, tiles_per_seq=tiles_per_seq, offs=offs)
    return pl.pallas_call(
        kern,
        grid=(n // tm,),
        in_specs=[
            pl.BlockSpec((tm, d_model), row),
            _layer_vec_spec(r_pre, d_model),
            _const_spec((None, d_model, offs[-1]), lambda i: (e, 0, 0)),
            _const_spec((None, w_t.shape[1], d_model), lambda i: (e, 0, 0)),
            _const_spec((None, 1, LANES), lambda i: (e, 0, 0)),
            _const_spec((None, SC_K, sc_w), lambda i: (e, 0, 0)),
            _const_spec((tm, tm), lambda i: (0, 0)),
            _const_spec(pqt.shape, lambda i: (0, 0)),
            _const_spec((LANES, q_rows), lambda i: (0, 0)),
        ],
        out_specs=[
            pl.BlockSpec((None, tm // ATT_TQ, q_rows, ATT_TQ), chunked),
            pl.BlockSpec((tm, q_rows), row),
            pl.BlockSpec((None, tm // ATT_TK, vt_rows, ATT_TK), chunked),
            pl.BlockSpec((tm, sc_w), row),
        ],
        out_shape=[
            jax.ShapeDtypeStruct((n // seq, seq // ATT_TQ, q_rows, ATT_TQ), BF16),
            jax.ShapeDtypeStruct((n, q_rows), BF16),
            jax.ShapeDtypeStruct((n // seq, seq // ATT_TK, vt_rows, ATT_TK), BF16),
            jax.ShapeDtypeStruct((n, sc_w), BF16),
        ],
        scratch_shapes=[pltpu.VMEM((1, LANES), F32), pltpu.VMEM((SUBLANES, sc_w), F32)],
        compiler_params=_params(("arbitrary",)),
        name="even_in",
    )(x2, vecs, w_all, w_t, bf_pad, conv_w, tri, pqt, pk)


def _fox_attn(qt, ka, vt, seq):
    bsz, nq, _, _ = qt.shape
    _, nkb, vt_rows, _ = vt.shape
    hps = ATT_HEADS_PER_STEP
    n_groups = vt_rows // (hps * VT_SLOT)
    return pl.pallas_call(
        _fox_attn_kernel,
        grid=(bsz, n_groups, nq),
        in_specs=[
            pl.BlockSpec((None, None, hps * HEAD_SLOT, ATT_TQ), lambda b, p, i: (b, i, p, 0)),
            pl.BlockSpec((seq, hps * HEAD_SLOT), lambda b, p, i: (b, p)),
            pl.BlockSpec((None, nkb, hps * VT_SLOT, ATT_TK), lambda b, p, i: (b, 0, p, 0)),
        ],
        out_specs=pl.BlockSpec((ATT_TQ, hps * FOX_HEAD_DIM), lambda b, p, i: (b * nq + i, p)),
        out_shape=jax.ShapeDtypeStruct((bsz * seq, n_groups * hps * FOX_HEAD_DIM), BF16),
        scratch_shapes=[pltpu.VMEM((hps, VT_SLOT, ATT_TQ), F32)],
        compiler_params=_params(("arbitrary", "arbitrary", "arbitrary")),
        name="fox_attn",
    )(qt, ka, vt)


def _lru(x2, vecs, rows, w_in, conv_w, w_a, w_i, o, seq):
    r_pre, r_cb, r_ba, r_bi, r_lam = rows
    n, d_model = x2.shape
    bsz = n // seq
    assert bsz == SUBLANES
    steps = LRU_STEPS
    nb = d_model // LRU_BW
    blk = pl.BlockSpec((bsz, steps, d_model), lambda i: (0, i, 0))
    relayout = pltpu.VMEM((d_model // LANES, bsz * LRU_PITCH, LANES), F32)
    y = pl.pallas_call(
        _lru_kernel,
        grid=(seq // steps,),
        in_specs=[
            blk,
            _layer_vec_spec(r_pre, d_model),
            _const_spec((None, d_model, 2 * d_model), lambda i: (o, 0, 0)),
            _const_spec((None, RG_CONV_K, d_model), lambda i: (o, 0, 0)),
            _layer_vec_spec(r_cb, d_model),
            _const_spec((None, nb, LRU_BW, LRU_BW), lambda i: (o, 0, 0, 0)),
            _layer_vec_spec(r_ba, d_model),
            _const_spec((None, nb, LRU_BW, LRU_BW), lambda i: (o, 0, 0, 0)),
            _layer_vec_spec(r_bi, d_model),
            _layer_vec_spec(r_lam, d_model),
        ],
        out_specs=blk,
        out_shape=jax.ShapeDtypeStruct((bsz, seq, d_model), BF16),
        scratch_shapes=[pltpu.VMEM((bsz, d_model), F32),
                        pltpu.VMEM(((RG_CONV_K - 1) * bsz, d_model), F32),
                        relayout, relayout],
        compiler_params=_params(("arbitrary",)),
        name="rglru",
    )(x2.reshape(bsz, seq, d_model), vecs, w_in, conv_w, vecs, w_a, vecs, w_i, vecs, vecs)
    return y.reshape(n, d_model)


def _mem_kv(mem, vecs, row0, w_xkv):
    bsz, mlen, d_model = mem.shape
    depth = w_xkv.shape[0]
    nb = bsz // MEM_KV_SPLIT
    return pl.pallas_call(
        _mem_kv_kernel,
        grid=(depth, MEM_KV_SPLIT),
        in_specs=[
            pl.BlockSpec((nb, mlen, d_model), lambda l, b: (b, 0, 0)),
            pl.BlockSpec((None, 1, d_model), lambda l, b: (row0 + l, 0, 0)),
            pl.BlockSpec((None, d_model, 2 * d_model), lambda l, b: (l, 0, 0)),
        ],
        out_specs=[
            pl.BlockSpec((None, nb, d_model, mlen), lambda l, b: (l, b, 0, 0)),
            pl.BlockSpec((None, nb, mlen, d_model), lambda l, b: (l, b, 0, 0)),
        ],
        out_shape=[jax.ShapeDtypeStruct((depth, bsz, d_model, mlen), BF16),
                   jax.ShapeDtypeStruct((depth, bsz, mlen, d_model), BF16)],
        compiler_params=_params(("arbitrary", "arbitrary")),
        name="mem_kv",
    )(mem, vecs, w_xkv)


def _cross(y_parts, w_parts, x2, layer, vecs, rows, w_xq, k_mem, v_mem, w_xo, seq):
    n, d_model = x2.shape
    r_post, r_cpre, r_cpost, r_fpre = rows
    tm = CROSS_TILE
    tiles_per_seq = seq // tm
    mlen = v_mem.shape[2]
    row = lambda i: (i, 0)
    in_specs = [pl.BlockSpec((tm, yp.shape[1]), row) for yp in y_parts]
    w_arrays = []
    for yp, (w_arr, idx, blk) in zip(y_parts, w_parts):
        in_specs.append(_const_spec((None, yp.shape[1], d_model),
                                    lambda i, idx=idx, blk=blk: (idx, blk, 0)))
        w_arrays.append(w_arr)
    mem_index = lambda i: (layer, i // tiles_per_seq, 0, 0)
    in_specs += [
        pl.BlockSpec((tm, d_model), row),
        _layer_vec_spec(r_post, d_model),
        _layer_vec_spec(r_cpre, d_model),
        _const_spec((None, d_model, d_model), lambda i: (layer, 0, 0)),
        pl.BlockSpec((None, None, d_model, mlen), mem_index),
        pl.BlockSpec((None, None, mlen, d_model), mem_index),
        _const_spec((None, d_model, d_model), lambda i: (layer, 0, 0)),
        _layer_vec_spec(r_cpost, d_model),
        _layer_vec_spec(r_fpre, d_model),
    ]
    kern = functools.partial(_cross_kernel, n_parts=len(y_parts))
    return pl.pallas_call(
        kern,
        grid=(n // tm,),
        in_specs=in_specs,
        out_specs=[pl.BlockSpec((tm, d_model), row), pl.BlockSpec((tm, d_model), row)],
        out_shape=[jax.ShapeDtypeStruct((n, d_model), F32),
                   jax.ShapeDtypeStruct((n, d_model), BF16)],
        compiler_params=_params(("arbitrary",)),
        name="cross",
    )(*y_parts, *w_arrays, x2, vecs, vecs, w_xq, k_mem, v_mem, w_xo, vecs, vecs)


def _ffn(h, x2, layer, w_gu, w_down, vecs, r_post):
    n, d_model = x2.shape
    d_ff = w_down.shape[1]
    tm = ROW_TILE
    row = lambda i: (i, 0)
    return pl.pallas_call(
        _ffn_kernel,
        grid=(n // tm,),
        in_specs=[
            pl.BlockSpec((tm, d_model), row),
            pl.BlockSpec((tm, d_model), row),
            _const_spec((None, d_model, d_ff), lambda i: (layer, 0, 0)),
            _const_spec((None, d_model, d_ff), lambda i: (layer, 0, 1)),
            _const_spec((None, d_ff, d_model), lambda i: (layer, 0, 0)),
            _layer_vec_spec(r_post, d_model),
        ],
        out_specs=pl.BlockSpec((tm, d_model), row),
        out_shape=jax.ShapeDtypeStruct((n, d_model), F32),
        compiler_params=_params(("arbitrary",)),
        name="ffn",
    )(h, x2, w_gu, w_gu, w_down, vecs)


def _pad_heads(w, n_heads, slot):
    e, d, _ = w.shape
    w = w.reshape(e, d, n_heads, FOX_HEAD_DIM)
    w = jnp.pad(w, ((0, 0), (0, 0), (0, 0), (0, slot - FOX_HEAD_DIM)))
    return w.reshape(e, d, n_heads * slot)


def _bias_placement(n_heads):
    pad_rows = VT_SLOT - FOX_HEAD_DIM
    pqt = [[0.0] * LANES for _ in range(n_heads * pad_rows)]
    pk = [[0.0] * (n_heads * HEAD_SLOT) for _ in range(LANES)]
    one_lane = 24
    for h in range(n_heads):
        base = h * HEAD_SLOT + FOX_HEAD_DIM
        for part in range(3):
            pqt[h * pad_rows + part][part * 8 + h] = 1.0
            pk[one_lane][base + part] = 1.0
            pqt[h * pad_rows + 3 + part][one_lane] = 1.0
            pk[part * 8 + h][base + 3 + part] = -1.0
    return jnp.array(pqt, BF16), jnp.array(pk, BF16)


def kernel(x, mem, g_mix_pre, g_mix_post, g_cross_pre, g_mem, g_cross_post, g_ffn_pre, g_ffn_post, w_xq, w_xkv, w_xo, w_ffn_gu, w_ffn_down, ab_w_in, ab_b_f, ab_conv_w, ab_w_out, c_w_in, c_conv_w, c_conv_b, c_w_a, c_b_a, c_w_i, c_b_i, c_lam, c_w_out):
    bsz, seq, d_model = x.shape
    depth = g_mix_pre.shape[0]
    n = bsz * seq
    n_heads = ab_b_f.shape[1]
    fox_w = n_heads * FOX_HEAD_DIM
    sc_w = ab_conv_w.shape[2]
    assert n_heads <= 8 and seq % ROW_TILE == 0 and seq % LRU_STEPS == 0 and seq % ATT_TQ == 0

    n_odd = c_lam.shape[0]
    gains = (g_mix_pre, g_mix_post, g_cross_pre, g_mem, g_cross_post, g_ffn_pre, g_ffn_post)
    odd_vecs = (c_conv_b, c_b_a.reshape(n_odd, -1), c_b_i.reshape(n_odd, -1), c_lam)
    vecs = jnp.concatenate(gains + odd_vecs, axis=0)[:, None, :]
    (R_MIX_PRE, R_MIX_POST, R_CROSS_PRE, R_MEM, R_CROSS_POST, R_FFN_PRE,
     R_FFN_POST) = (k * depth for k in range(len(gains)))
    R_CONV_B, R_B_A, R_B_I, R_LAM = (len(gains) * depth + k * n_odd for k in range(4))

    i1, i2, i3 = fox_w, 2 * fox_w, 3 * fox_w
    i4 = i3 + n_heads
    i5, i6 = i4 + sc_w, i4 + 2 * sc_w
    w_in_b = ab_w_in.astype(BF16)
    w_f = w_in_b[:, :, i3:i4]
    w_k = _pad_heads(w_in_b[:, :, i1:i2], n_heads, HEAD_SLOT)
    w_k = w_k.at[:, :, F_LANE0:F_LANE0 + 3 * n_heads].set(jnp.concatenate([w_f, w_f, w_f], -1))
    pieces = [w_k, w_in_b[:, :, i4:i5], w_in_b[:, :, i5:i6], w_in_b[:, :, i6:]]
    w_t = jnp.concatenate([w_in_b[:, :, 0:i1], w_in_b[:, :, i2:i3]], axis=-1)
    w_t = w_t.transpose(0, 2, 1)
    offs = []
    for p in pieces:
        offs.append((offs[-1] if offs else 0) + p.shape[-1])
    offs = tuple(offs)
    w_even = jnp.concatenate(pieces, axis=-1)
    b_f3 = jnp.pad(jnp.concatenate([ab_b_f, ab_b_f, ab_b_f], axis=-1),
                   ((0, 0), (0, LANES - 3 * n_heads))).reshape(-1, 1, LANES)
    tri = (lax.broadcasted_iota(jnp.int32, (ROW_TILE, ROW_TILE), 0)
           >= lax.broadcasted_iota(jnp.int32, (ROW_TILE, ROW_TILE), 1)).astype(BF16)
    pqt, pk = _bias_placement(n_heads)

    w_out_even = ab_w_out.astype(BF16)
    w_c_in = c_w_in.astype(BF16)
    w_c_out = c_w_out.astype(BF16)
    w_a = c_w_a.astype(BF16)
    w_i = c_w_i.astype(BF16)
    w_xq_b = w_xq.astype(BF16)
    w_xo_b = w_xo.astype(BF16)
    w_gu_b = w_ffn_gu.astype(BF16)
    w_down_b = w_ffn_down.astype(BF16)

    k_mem, v_mem = _mem_kv(mem, vecs, R_MEM, w_xkv.astype(BF16))

    x2 = x.reshape(n, d_model)
    for layer in range(depth):
        if layer % 2 == 0:
            e = layer // 2
            qt, ka, vt, yb = _even_in(x2, vecs, R_MIX_PRE + layer, w_even, w_t, b_f3,
                                      ab_conv_w, e, tri, pqt, pk, seq, offs)
            ya = _fox_attn(qt, ka, vt, seq)
            y_parts = [ya, yb]
            assert fox_w == sc_w
            w_parts = [(w_out_even, e, 0), (w_out_even, e, 1)]
        else:
            o = layer // 2
            rows = (R_MIX_PRE + layer, R_CONV_B + o, R_B_A + o, R_B_I + o, R_LAM + o)
            ym = _lru(x2, vecs, rows, w_c_in, c_conv_w, w_a, w_i, o, seq)
            y_parts = [ym]
            w_parts = [(w_c_out, o, 0)]
        rows = (R_MIX_POST + layer, R_CROSS_PRE + layer, R_CROSS_POST + layer,
                R_FFN_PRE + layer)
        x2, h = _cross(y_parts, w_parts, x2, layer, vecs, rows, w_xq_b, k_mem, v_mem, w_xo_b,
                       seq)
        x2 = _ffn(h, x2, layer, w_gu_b, w_down_b, vecs, R_FFN_POST + layer)
    return x2.reshape(bsz, seq, d_model)
```

```python
import functools
import math

import jax
import jax.numpy as jnp
from jax import lax
from jax.experimental import pallas as pl
from jax.experimental.pallas import tpu as pltpu

F32 = jnp.float32
BF16 = jnp.bfloat16

EPS = 1e-6
NEG_INF = -1e30
FOX_HEAD_DIM = 64
SC_K = 3
LRU_BW = 256
RG_CONV_K = 4
RG_C = 8.0
MEM_HEADS = 4

LANES = 128
SUBLANES = 8
HEAD_SLOT = LANES
VT_SLOT = 80
F_LANE0 = FOX_HEAD_DIM + 6
LOG2_E = math.log2(math.e)
VMEM_LIMIT = 56 * 1024 * 1024

ROW_TILE = 512
CROSS_TILE = 1024
CROSS_SPLIT = 4
MEM_KV_SPLIT = 2
LRU_STEPS = 64
LRU_PITCH = LRU_STEPS + SUBLANES
ATT_TQ = 256
ATT_TK = 256
ATT_HEADS_PER_STEP = 8
FFN_TILE = 512
FFN_CHUNKS = ((0, 1024), (1024, 1024), (2048, 768))

NT_DIMS = (((1,), (1,)), ((), ()))


def _dot(a, b):
    return jnp.dot(a, b, preferred_element_type=F32)


def _dot_nt(a, b):
    return lax.dot_general(a, b, NT_DIMS, preferred_element_type=F32)


def _rms(x, g):
    ms = jnp.mean(x * x, axis=-1, keepdims=True)
    return x * lax.rsqrt(ms + EPS) * g


def _split3(x):
    hi = x.astype(BF16).astype(F32)
    r = x - hi
    mid = r.astype(BF16).astype(F32)
    lo = (r - mid).astype(BF16).astype(F32)
    return hi, mid, lo


def _causal_conv(cur, prev_tail, w_ref, k_width):
    head_cur = cur[0:SUBLANES, :]
    row8 = lax.broadcasted_iota(jnp.int32, head_cur.shape, 0)
    main = cur * w_ref[k_width - 1:k_width, :]
    head = head_cur * w_ref[k_width - 1:k_width, :]
    for d in range(1, k_width):
        wk = w_ref[k_width - 1 - d:k_width - d, :]
        main = main + pltpu.roll(cur, d, 0) * wk
        shifted = jnp.where(row8 < d, pltpu.roll(prev_tail, d, 0), pltpu.roll(head_cur, d, 0))
        head = head + shifted * wk
    return main, head


def _even_in_kernel(x_ref, g_ref, w_ref, wt_ref, bf_ref, cw_ref, tri_ref,
                    pqt_ref, pk_ref, qt_ref, ka_ref, vt_ref, yb_ref, carry_c, carry_cu, *,
                    tiles_per_seq, offs):
    o_b, o_c, o_u, o_end = offs
    tm = x_ref.shape[0]
    fox_w = wt_ref.shape[0] // 2
    n_heads = fox_w // FOX_HEAD_DIM
    first = (pl.program_id(0) % tiles_per_seq) == 0

    @pl.when(first)
    def _():
        carry_c[...] = jnp.zeros_like(carry_c)
        carry_cu[...] = jnp.zeros_like(carry_cu)

    h = _rms(x_ref[...], g_ref[...]).astype(BF16)

    lane = lax.broadcasted_iota(jnp.int32, (tm, LANES), 1)
    kp = _dot(h, w_ref[:, 0:o_b])
    fl = pltpu.roll(kp[:, 0:LANES], LANES - F_LANE0, 1)
    fl = jnp.where(lane < 24, fl, 0.0) + bf_ref[...]

    lf = jax.nn.log_sigmoid(fl)
    hi, mid, lo = _split3(lf)
    parts = jnp.where(lane < 8, hi, jnp.where(lane < 16, mid, lo)).astype(BF16)
    r = _dot(tri_ref[...], parts)
    cum = r + pltpu.roll(r, LANES - 8, 1) + pltpu.roll(r, LANES - 16, 1)
    cum = jnp.where(lane < 8, cum + carry_c[...], 0.0)
    carry_c[...] = cum[tm - 1:tm, :]
    chi, cmid, clo = _split3(cum * LOG2_E)
    sel = jnp.where(lane < 8, chi,
                    jnp.where(lane < 16, pltpu.roll(cmid, 8, 1),
                              jnp.where(lane < 24, pltpu.roll(clo, 16, 1),
                                        jnp.where(lane == 24, 1.0, 0.0)))).astype(BF16)

    k0 = jnp.where(lane < FOX_HEAD_DIM, kp[:, 0:LANES], 0.0)
    kp = jnp.concatenate([k0, kp[:, LANES:]], axis=-1)
    ka_ref[...] = (kp + _dot(sel, pk_ref[...])).astype(BF16)

    scale = FOX_HEAD_DIM ** -0.5 * LOG2_E
    qt = _dot_nt(wt_ref[0:fox_w, :], h) * scale
    qb = _dot_nt(pqt_ref[...], sel)
    vt = _dot_nt(wt_ref[fox_w:, :], h)
    pad_rows = VT_SLOT - FOX_HEAD_DIM
    q_zero = jnp.zeros((HEAD_SLOT - FOX_HEAD_DIM - pad_rows, tm), F32)
    ones_blk = (lax.broadcasted_iota(jnp.int32, (pad_rows, tm), 0) == 0).astype(F32)
    q_pieces, v_pieces = [], []
    for n in range(n_heads):
        feat = slice(n * FOX_HEAD_DIM, (n + 1) * FOX_HEAD_DIM)
        q_pieces += [qt[feat, :], qb[n * pad_rows:(n + 1) * pad_rows, :], q_zero]
        v_pieces += [vt[feat, :], ones_blk]
    qt = jnp.concatenate(q_pieces, axis=0).astype(BF16)
    vt = jnp.concatenate(v_pieces, axis=0).astype(BF16)
    for c in range(qt_ref.shape[0]):
        w = qt_ref.shape[-1]
        qt_ref[c] = qt[:, c * w:(c + 1) * w]
    for c in range(vt_ref.shape[0]):
        w = vt_ref.shape[-1]
        vt_ref[c] = vt[:, c * w:(c + 1) * w]

    b_gate = _dot(h, w_ref[:, o_b:o_c])
    cu = _dot(h, w_ref[:, o_c:o_u]) * _dot(h, w_ref[:, o_u:o_end])
    main, head = _causal_conv(cu, carry_cu[...], cw_ref, SC_K)
    carry_cu[...] = cu[tm - SUBLANES:tm, :]
    yb_ref[...] = (b_gate * main).astype(BF16)
    yb_ref[0:SUBLANES, :] = (b_gate[0:SUBLANES, :] * head).astype(BF16)


def _fox_attn_kernel(qt_ref, ka_ref, vt_ref, o_ref, acc_ref):
    tq = qt_ref.shape[-1]
    qi = pl.program_id(2)
    krow = lax.broadcasted_iota(jnp.int32, (ATT_TK, tq), 0)
    qcol = lax.broadcasted_iota(jnp.int32, (ATT_TK, tq), 1)
    heads = range(acc_ref.shape[0])

    def step(j, ms, masked):
        k0 = pl.multiple_of(j * ATT_TK, ATT_TK)
        sts = [_dot(ka_ref[pl.ds(k0, ATT_TK), hh * HEAD_SLOT:(hh + 1) * HEAD_SLOT],
                    qt_ref[hh * HEAD_SLOT:(hh + 1) * HEAD_SLOT, :]) for hh in heads]
        ps, alphas, new = [], [], []
        for hh in heads:
            st = sts[hh]
            if masked:
                st = jnp.where(qcol >= krow, st, NEG_INF)
            m_new = jnp.maximum(ms[hh], jnp.max(st, axis=0, keepdims=True))
            alphas.append(jnp.exp2(ms[hh] - m_new))
            ps.append(jnp.exp2(st - m_new).astype(BF16))
            new.append(m_new)
        for hh in heads:
            vt = vt_ref[j, hh * VT_SLOT:(hh + 1) * VT_SLOT, :]
            acc_ref[hh] = alphas[hh] * acc_ref[hh] + _dot(vt, ps[hh])
        return tuple(new)

    acc_ref[...] = jnp.zeros_like(acc_ref)
    ms = tuple(jnp.full((1, tq), NEG_INF, F32) for _ in heads)
    ms = lax.fori_loop(0, qi, functools.partial(step, masked=False), ms)
    step(qi, ms, True)
    o_t = jnp.concatenate(
        [acc_ref[hh, 0:FOX_HEAD_DIM, :] / acc_ref[hh, FOX_HEAD_DIM:FOX_HEAD_DIM + 1, :]
         for hh in heads], axis=0)
    o_ref[...] = o_t.T.astype(BF16)


def _lru_kernel(x_ref, g_ref, win_ref, cw_ref, cb_ref, wa_ref, ba_ref, wi_ref, bi_ref,
                lam_ref, y_ref, carry_h, carry_u, in_scr, out_scr):
    bsz, steps, width = x_ref.shape
    tm = steps * bsz
    tail = (RG_CONV_K - 1) * bsz
    lane_tiles = range(width // LANES)

    @pl.when(pl.program_id(0) == 0)
    def _():
        carry_h[...] = jnp.zeros_like(carry_h)
        carry_u[...] = jnp.zeros_like(carry_u)

    for b in range(bsz):
        hb = _rms(x_ref[b], g_ref[...])
        for k in lane_tiles:
            in_scr[k, b * LRU_PITCH:b * LRU_PITCH + steps, :] = hb[:, k * LANES:(k + 1) * LANES]
    h = jnp.concatenate(
        [jnp.concatenate([in_scr[k, pl.ds(t, bsz, stride=LRU_PITCH), :] for k in lane_tiles],
                         axis=-1) for t in range(steps)], axis=0).astype(BF16)
    u = _dot(h, win_ref[:, width:2 * width])

    ext = jnp.concatenate([carry_u[...], u], axis=0)
    carry_u[...] = u[tm - tail:tm, :]
    uc = cb_ref[...] + u * cw_ref[RG_CONV_K - 1:RG_CONV_K, :]
    for d in range(1, RG_CONV_K):
        shifted = ext[tail - d * bsz:tail - d * bsz + tm, :]
        uc = uc + shifted * cw_ref[RG_CONV_K - 1 - d:RG_CONV_K - d, :]
    ucb = uc.astype(BF16)

    nb = width // LRU_BW
    r = jnp.concatenate(
        [_dot(ucb[:, n * LRU_BW:(n + 1) * LRU_BW], wa_ref[n]) for n in range(nb)], axis=-1)
    r = jax.nn.sigmoid(r + ba_ref[...])
    gi = jnp.concatenate(
        [_dot(ucb[:, n * LRU_BW:(n + 1) * LRU_BW], wi_ref[n]) for n in range(nb)], axis=-1)
    gi = jax.nn.sigmoid(gi + bi_ref[...])

    log_a = r * (-RG_C * jax.nn.softplus(-lam_ref[...]))
    a = jnp.exp(log_a)
    b = jnp.sqrt(-jnp.tanh(log_a) * (1.0 + a * a)) * (gi * uc)

    state = carry_h[...]
    hs = []
    for t in range(steps):
        state = a[t * bsz:(t + 1) * bsz, :] * state + b[t * bsz:(t + 1) * bsz, :]
        hs.append(state)
    carry_h[...] = state
    hs = jnp.concatenate(hs, axis=0)

    y = jax.nn.gelu(_dot(h, win_ref[:, 0:width])) * hs
    for t in range(steps):
        for k in lane_tiles:
            out_scr[k, pl.ds(t, bsz, stride=LRU_PITCH), :] = (
                y[t * bsz:(t + 1) * bsz, k * LANES:(k + 1) * LANES])
    for b in range(bsz):
        y_ref[b] = jnp.concatenate(
            [out_scr[k, b * LRU_PITCH:b * LRU_PITCH + steps, :] for k in lane_tiles],
            axis=-1).astype(BF16)


def _mem_kv_kernel(mem_ref, g_ref, w_ref, kt_ref, v_ref):
    nb, mlen, d_model = mem_ref.shape
    m = _rms(mem_ref[...].reshape(nb * mlen, d_model), g_ref[...]).astype(BF16)
    k = _dot(m, w_ref[:, 0:d_model])
    v = _dot(m, w_ref[:, d_model:2 * d_model]).astype(BF16)
    for b in range(nb):
        kt_ref[b] = k[b * mlen:(b + 1) * mlen, :].T.astype(BF16)
        v_ref[b] = v[b * mlen:(b + 1) * mlen, :]


def _cross_kernel(*refs, n_parts):
    y_refs = refs[0:n_parts]
    w_refs = refs[n_parts:2 * n_parts]
    (x_ref, gpost_ref, gcpre_ref, wq_ref, kt_ref, v_ref, wo_ref, gcpost_ref, gfpre_ref,
     wgu_ref, wd_ref, xo_ref, ho_ref, wgu_out_ref, wd_out_ref) = refs[2 * n_parts:]
    wgu_out_ref[...] = wgu_ref[...].astype(BF16)
    wd_out_ref[...] = wd_ref[...].astype(BF16)
    tm, d_model = x_ref.shape
    hd = d_model // MEM_HEADS
    sub = tm // CROSS_SPLIT
    rows = [pl.ds(s * sub, sub) for s in range(CROSS_SPLIT)]
    each = lambda f, *lists: [f(*args) for args in zip(*lists)]

    def mix_out(r):
        y = _dot(y_refs[0][r, :], w_refs[0][...])
        for p in range(1, n_parts):
            y = y + _dot(y_refs[p][r, :], w_refs[p][...])
        return y

    ys = each(mix_out, rows)
    x1s = each(lambda r, y: x_ref[r, :] + _rms(y, gpost_ref[...]), rows, ys)
    hs = each(lambda x1: _rms(x1, gcpre_ref[...]).astype(BF16), x1s)
    qs = each(lambda h: (_dot(h, wq_ref[...]) * (hd ** -0.5)).astype(BF16), hs)
    outs = [[] for _ in rows]
    for n in range(MEM_HEADS):
        ss = each(lambda q: _dot(q[:, n * hd:(n + 1) * hd], kt_ref[n * hd:(n + 1) * hd, :]), qs)
        es = each(lambda s: jnp.exp(s - jnp.max(s, axis=-1, keepdims=True)), ss)
        os_ = each(lambda e: _dot(e.astype(BF16), v_ref[:, n * hd:(n + 1) * hd])
                   / jnp.sum(e, axis=-1, keepdims=True), es)
        for lst, o in zip(outs, os_):
            lst.append(o.astype(BF16))
    y2s = each(lambda o: _dot(jnp.concatenate(o, axis=-1), wo_ref[...]), outs)
    x2s = each(lambda x1, y2: x1 + _rms(y2, gcpost_ref[...]), x1s, y2s)
    for r, x2 in zip(rows, x2s):
        xo_ref[r, :] = x2
        ho_ref[r, :] = _rms(x2, gfpre_ref[...]).astype(BF16)


def _ffn_kernel(h_ref, x_ref, wg_ref, wu_ref, wd_ref, g_ref, xo_ref):
    h = h_ref[...]
    acc = None
    for c0, cw in FFN_CHUNKS:
        a = (jax.nn.silu(_dot(h, wg_ref[:, c0:c0 + cw])) * _dot(h, wu_ref[:, c0:c0 + cw]))
        part = _dot(a.astype(BF16), wd_ref[c0:c0 + cw, :])
        acc = part if acc is None else acc + part
    xo_ref[...] = x_ref[...] + _rms(acc, g_ref[...])


def _params(sem):
    return pltpu.CompilerParams(dimension_semantics=sem, vmem_limit_bytes=VMEM_LIMIT)


def _const_spec(shape, index):
    return pl.BlockSpec(shape, index, pipeline_mode=pl.Buffered(1))


def _layer_vec_spec(row, width):
    return _const_spec((None, 1, width), lambda *_: (row, 0, 0))


def _even_in(x2, vecs, r_pre, w_all, w_t, bf_pad, conv_w, e, tri, pqt, pk, seq, offs):
    n, d_model = x2.shape
    tm = ROW_TILE
    sc_w = offs[2] - offs[1]
    q_rows = pk.shape[1]
    n_heads = q_rows // HEAD_SLOT
    vt_rows = n_heads * VT_SLOT
    tiles_per_seq = seq // tm
    row = lambda i: (i, 0)
    chunked = lambda i: (i // tiles_per_seq, i % tiles_per_seq, 0, 0)
    kern = functools.partial(_even_in_kernel, tiles_per_seq=tiles_per_seq, offs=offs)
    return pl.pallas_call(
        kern,
        grid=(n // tm,),
        in_specs=[
            pl.BlockSpec((tm, d_model), row),
            _layer_vec_spec(r_pre, d_model),
            _const_spec((None, d_model, offs[-1]), lambda i: (e, 0, 0)),
            _const_spec((None, w_t.shape[1], d_model), lambda i: (e, 0, 0)),
            _const_spec((None, 1, LANES), lambda i: (e, 0, 0)),
            _const_spec((None, SC_K, sc_w), lambda i: (e, 0, 0)),
            _const_spec((tm, tm), lambda i: (0, 0)),
            _const_spec(pqt.shape, lambda i: (0, 0)),
            _const_spec((LANES, q_rows), lambda i: (0, 0)),
        ],
        out_specs=[
            pl.BlockSpec((None, tm // ATT_TQ, q_rows, ATT_TQ), chunked),
            pl.BlockSpec((tm, q_rows), row),
            pl.BlockSpec((None, tm // ATT_TK, vt_rows, ATT_TK), chunked),
            pl.BlockSpec((tm, sc_w), row),
        ],
        out_shape=[
            jax.ShapeDtypeStruct((n // seq, seq // ATT_TQ, q_rows, ATT_TQ), BF16),
            jax.ShapeDtypeStruct((n, q_rows), BF16),
            jax.ShapeDtypeStruct((n // seq, seq // ATT_TK, vt_rows, ATT_TK), BF16),
            jax.ShapeDtypeStruct((n, sc_w), BF16),
        ],
        scratch_shapes=[pltpu.VMEM((1, LANES), F32), pltpu.VMEM((SUBLANES, sc_w), F32)],
        compiler_params=_params(("arbitrary",)),
        name="even_in",
    )(x2, vecs, w_all, w_t, bf_pad, conv_w, tri, pqt, pk)


def _fox_attn(qt, ka, vt, seq):
    bsz, nq, _, _ = qt.shape
    _, nkb, vt_rows, _ = vt.shape
    hps = ATT_HEADS_PER_STEP
    n_groups = vt_rows // (hps * VT_SLOT)
    return pl.pallas_call(
        _fox_attn_kernel,
        grid=(bsz, n_groups, nq),
        in_specs=[
            pl.BlockSpec((None, None, hps * HEAD_SLOT, ATT_TQ), lambda b, p, i: (b, i, p, 0)),
            pl.BlockSpec((seq, hps * HEAD_SLOT), lambda b, p, i: (b, p)),
            pl.BlockSpec((None, nkb, hps * VT_SLOT, ATT_TK), lambda b, p, i: (b, 0, p, 0)),
        ],
        out_specs=pl.BlockSpec((ATT_TQ, hps * FOX_HEAD_DIM), lambda b, p, i: (b * nq + i, p)),
        out_shape=jax.ShapeDtypeStruct((bsz * seq, n_groups * hps * FOX_HEAD_DIM), BF16),
        scratch_shapes=[pltpu.VMEM((hps, VT_SLOT, ATT_TQ), F32)],
        compiler_params=_params(("arbitrary", "arbitrary", "arbitrary")),
        name="fox_attn",
    )(qt, ka, vt)


def _lru(x2, vecs, rows, w_in, conv_w, w_a, w_i, o, seq):
    r_pre, r_cb, r_ba, r_bi, r_lam = rows
    n, d_model = x2.shape
    bsz = n // seq
    assert bsz == SUBLANES
    steps = LRU_STEPS
    nb = d_model // LRU_BW
    blk = pl.BlockSpec((bsz, steps, d_model), lambda i: (0, i, 0))
    relayout = pltpu.VMEM((d_model // LANES, bsz * LRU_PITCH, LANES), F32)
    y = pl.pallas_call(
        _lru_kernel,
        grid=(seq // steps,),
        in_specs=[
            blk,
            _layer_vec_spec(r_pre, d_model),
            _const_spec((None, d_model, 2 * d_model), lambda i: (o, 0, 0)),
            _const_spec((None, RG_CONV_K, d_model), lambda i: (o, 0, 0)),
            _layer_vec_spec(r_cb, d_model),
            _const_spec((None, nb, LRU_BW, LRU_BW), lambda i: (o, 0, 0, 0)),
            _layer_vec_spec(r_ba, d_model),
            _const_spec((None, nb, LRU_BW, LRU_BW), lambda i: (o, 0, 0, 0)),
            _layer_vec_spec(r_bi, d_model),
            _layer_vec_spec(r_lam, d_model),
        ],
        out_specs=blk,
        out_shape=jax.ShapeDtypeStruct((bsz, seq, d_model), BF16),
        scratch_shapes=[pltpu.VMEM((bsz, d_model), F32),
                        pltpu.VMEM(((RG_CONV_K - 1) * bsz, d_model), F32),
                        relayout, relayout],
        compiler_params=_params(("arbitrary",)),
        name="rglru",
    )(x2.reshape(bsz, seq, d_model), vecs, w_in, conv_w, vecs, w_a, vecs, w_i, vecs, vecs)
    return y.reshape(n, d_model)


def _mem_kv(mem, vecs, row0, w_xkv):
    bsz, mlen, d_model = mem.shape
    depth = w_xkv.shape[0]
    nb = bsz // MEM_KV_SPLIT
    return pl.pallas_call(
        _mem_kv_kernel,
        grid=(depth, MEM_KV_SPLIT),
        in_specs=[
            pl.BlockSpec((nb, mlen, d_model), lambda l, b: (b, 0, 0)),
            pl.BlockSpec((None, 1, d_model), lambda l, b: (row0 + l, 0, 0)),
            pl.BlockSpec((None, d_model, 2 * d_model), lambda l, b: (l, 0, 0)),
        ],
        out_specs=[
            pl.BlockSpec((None, nb, d_model, mlen), lambda l, b: (l, b, 0, 0)),
            pl.BlockSpec((None, nb, mlen, d_model), lambda l, b: (l, b, 0, 0)),
        ],
        out_shape=[jax.ShapeDtypeStruct((depth, bsz, d_model, mlen), BF16),
                   jax.ShapeDtypeStruct((depth, bsz, mlen, d_model), BF16)],
        compiler_params=_params(("arbitrary", "arbitrary")),
        name="mem_kv",
    )(mem, vecs, w_xkv)


def _cross(y_parts, w_parts, x2, layer, vecs, rows, w_xq, k_mem, v_mem, w_xo, w_gu, w_down,
           seq):
    n, d_model = x2.shape
    d_ff = w_down.shape[1]
    steps = n // CROSS_TILE
    gu_rows, down_rows = d_model // steps, d_ff // steps
    slab = lambda i: (i, 0)
    r_post, r_cpre, r_cpost, r_fpre = rows
    tm = CROSS_TILE
    tiles_per_seq = seq // tm
    mlen = v_mem.shape[2]
    row = lambda i: (i, 0)
    in_specs = [pl.BlockSpec((tm, yp.shape[1]), row) for yp in y_parts]
    w_arrays = []
    for yp, (w_arr, idx, blk) in zip(y_parts, w_parts):
        in_specs.append(_const_spec((None, yp.shape[1], d_model),
                                    lambda i, idx=idx, blk=blk: (idx, blk, 0)))
        w_arrays.append(w_arr)
    mem_index = lambda i: (layer, i // tiles_per_seq, 0, 0)
    in_specs += [
        pl.BlockSpec((tm, d_model), row),
        _layer_vec_spec(r_post, d_model),
        _layer_vec_spec(r_cpre, d_model),
        _const_spec((None, d_model, d_model), lambda i: (layer, 0, 0)),
        pl.BlockSpec((None, None, d_model, mlen), mem_index),
        pl.BlockSpec((None, None, mlen, d_model), mem_index),
        _const_spec((None, d_model, d_model), lambda i: (layer, 0, 0)),
        _layer_vec_spec(r_cpost, d_model),
        _layer_vec_spec(r_fpre, d_model),
        pl.BlockSpec((None, gu_rows, 2 * d_ff), lambda i: (layer, i, 0)),
        pl.BlockSpec((None, down_rows, d_model), lambda i: (layer, i, 0)),
    ]
    kern = functools.partial(_cross_kernel, n_parts=len(y_parts))
    return pl.pallas_call(
        kern,
        grid=(steps,),
        in_specs=in_specs,
        out_specs=[pl.BlockSpec((tm, d_model), row), pl.BlockSpec((tm, d_model), row),
                   pl.BlockSpec((gu_rows, 2 * d_ff), slab),
                   pl.BlockSpec((down_rows, d_model), slab)],
        out_shape=[jax.ShapeDtypeStruct((n, d_model), F32),
                   jax.ShapeDtypeStruct((n, d_model), BF16),
                   jax.ShapeDtypeStruct((d_model, 2 * d_ff), BF16),
                   jax.ShapeDtypeStruct((d_ff, d_model), BF16)],
        compiler_params=_params(("arbitrary",)),
        name="cross",
    )(*y_parts, *w_arrays, x2, vecs, vecs, w_xq, k_mem, v_mem, w_xo, vecs, vecs, w_gu, w_down)


def _ffn(h, x2, w_gu, w_down, vecs, r_post):
    n, d_model = x2.shape
    d_ff = w_down.shape[0]
    tm = FFN_TILE
    row = lambda i: (i, 0)
    return pl.pallas_call(
        _ffn_kernel,
        grid=(n // tm,),
        in_specs=[
            pl.BlockSpec((tm, d_model), row),
            pl.BlockSpec((tm, d_model), row),
            _const_spec((d_model, d_ff), lambda i: (0, 0)),
            _const_spec((d_model, d_ff), lambda i: (0, 1)),
            _const_spec((d_ff, d_model), lambda i: (0, 0)),
            _layer_vec_spec(r_post, d_model),
        ],
        out_specs=pl.BlockSpec((tm, d_model), row),
        out_shape=jax.ShapeDtypeStruct((n, d_model), F32),
        compiler_params=_params(("arbitrary",)),
        name="ffn",
    )(h, x2, w_gu, w_gu, w_down, vecs)


def _pad_heads(w, n_heads, slot):
    e, d, _ = w.shape
    w = w.reshape(e, d, n_heads, FOX_HEAD_DIM)
    w = jnp.pad(w, ((0, 0), (0, 0), (0, 0), (0, slot - FOX_HEAD_DIM)))
    return w.reshape(e, d, n_heads * slot)


def _bias_placement(n_heads):
    pad_rows = VT_SLOT - FOX_HEAD_DIM
    pqt = [[0.0] * LANES for _ in range(n_heads * pad_rows)]
    pk = [[0.0] * (n_heads * HEAD_SLOT) for _ in range(LANES)]
    one_lane = 24
    for h in range(n_heads):
        base = h * HEAD_SLOT + FOX_HEAD_DIM
        for part in range(3):
            pqt[h * pad_rows + part][part * 8 + h] = 1.0
            pk[one_lane][base + part] = 1.0
            pqt[h * pad_rows + 3 + part][one_lane] = 1.0
            pk[part * 8 + h][base + 3 + part] = -1.0
    return jnp.array(pqt, BF16), jnp.array(pk, BF16)


def kernel(x, mem, g_mix_pre, g_mix_post, g_cross_pre, g_mem, g_cross_post, g_ffn_pre, g_ffn_post, w_xq, w_xkv, w_xo, w_ffn_gu, w_ffn_down, ab_w_in, ab_b_f, ab_conv_w, ab_w_out, c_w_in, c_conv_w, c_conv_b, c_w_a, c_b_a, c_w_i, c_b_i, c_lam, c_w_out):
    bsz, seq, d_model = x.shape
    depth = g_mix_pre.shape[0]
    n = bsz * seq
    n_heads = ab_b_f.shape[1]
    fox_w = n_heads * FOX_HEAD_DIM
    sc_w = ab_conv_w.shape[2]
    assert n_heads <= 8 and seq % ROW_TILE == 0 and seq % LRU_STEPS == 0 and seq % ATT_TQ == 0

    n_odd = c_lam.shape[0]
    gains = (g_mix_pre, g_mix_post, g_cross_pre, g_mem, g_cross_post, g_ffn_pre, g_ffn_post)
    odd_vecs = (c_conv_b, c_b_a.reshape(n_odd, -1), c_b_i.reshape(n_odd, -1), c_lam)
    vecs = jnp.concatenate(gains + odd_vecs, axis=0)[:, None, :]
    (R_MIX_PRE, R_MIX_POST, R_CROSS_PRE, R_MEM, R_CROSS_POST, R_FFN_PRE,
     R_FFN_POST) = (k * depth for k in range(len(gains)))
    R_CONV_B, R_B_A, R_B_I, R_LAM = (len(gains) * depth + k * n_odd for k in range(4))

    i1, i2, i3 = fox_w, 2 * fox_w, 3 * fox_w
    i4 = i3 + n_heads
    i5, i6 = i4 + sc_w, i4 + 2 * sc_w
    w_in_b = ab_w_in.astype(BF16)
    w_f = w_in_b[:, :, i3:i4]
    w_k = _pad_heads(w_in_b[:, :, i1:i2], n_heads, HEAD_SLOT)
    w_k = w_k.at[:, :, F_LANE0:F_LANE0 + 3 * n_heads].set(jnp.concatenate([w_f, w_f, w_f], -1))
    pieces = [w_k, w_in_b[:, :, i4:i5], w_in_b[:, :, i5:i6], w_in_b[:, :, i6:]]
    w_t = jnp.concatenate([w_in_b[:, :, 0:i1], w_in_b[:, :, i2:i3]], axis=-1)
    w_t = w_t.transpose(0, 2, 1)
    offs = []
    for p in pieces:
        offs.append((offs[-1] if offs else 0) + p.shape[-1])
    offs = tuple(offs)
    w_even = jnp.concatenate(pieces, axis=-1)
    b_f3 = jnp.pad(jnp.concatenate([ab_b_f, ab_b_f, ab_b_f], axis=-1),
                   ((0, 0), (0, LANES - 3 * n_heads))).reshape(-1, 1, LANES)
    tri = (lax.broadcasted_iota(jnp.int32, (ROW_TILE, ROW_TILE), 0)
           >= lax.broadcasted_iota(jnp.int32, (ROW_TILE, ROW_TILE), 1)).astype(BF16)
    pqt, pk = _bias_placement(n_heads)

    w_out_even = ab_w_out.astype(BF16)
    w_c_in = c_w_in.astype(BF16)
    w_c_out = c_w_out.astype(BF16)
    w_a = c_w_a.astype(BF16)
    w_i = c_w_i.astype(BF16)
    w_xq_b = w_xq.astype(BF16)
    w_xo_b = w_xo.astype(BF16)

    k_mem, v_mem = _mem_kv(mem, vecs, R_MEM, w_xkv.astype(BF16))

    x2 = x.reshape(n, d_model)
    for layer in range(depth):
        if layer % 2 == 0:
            e = layer // 2
            qt, ka, vt, yb = _even_in(x2, vecs, R_MIX_PRE + layer, w_even, w_t, b_f3,
                                      ab_conv_w, e, tri, pqt, pk, seq, offs)
            ya = _fox_attn(qt, ka, vt, seq)
            y_parts = [ya, yb]
            assert fox_w == sc_w
            w_parts = [(w_out_even, e, 0), (w_out_even, e, 1)]
        else:
            o = layer // 2
            rows = (R_MIX_PRE + layer, R_CONV_B + o, R_B_A + o, R_B_I + o, R_LAM + o)
            ym = _lru(x2, vecs, rows, w_c_in, c_conv_w, w_a, w_i, o, seq)
            y_parts = [ym]
            w_parts = [(w_c_out, o, 0)]
        rows = (R_MIX_POST + layer, R_CROSS_PRE + layer, R_CROSS_POST + layer,
                R_FFN_PRE + layer)
        x2, h, w_gu_b, w_down_b = _cross(y_parts, w_parts, x2, layer, vecs, rows, w_xq_b,
                                         k_mem, v_mem, w_xo_b, w_ffn_gu, w_ffn_down, seq)
        x2 = _ffn(h, x2, w_gu_b, w_down_b, vecs, R_FFN_POST + layer)
    return x2.reshape(bsz, seq, d_model)
```

```python
import functools
import math

import jax
import jax.numpy as jnp
from jax import lax
from jax.experimental import pallas as pl
from jax.experimental.pallas import tpu as pltpu

F32 = jnp.float32
BF16 = jnp.bfloat16

EPS = 1e-6
NEG_INF = -1e30
FOX_HEAD_DIM = 64
SC_K = 3
LRU_BW = 256
RG_CONV_K = 4
RG_C = 8.0
MEM_HEADS = 4

LANES = 128
SUBLANES = 8
HEAD_SLOT = LANES
VT_SLOT = 80
F_LANE0 = FOX_HEAD_DIM + 6
LOG2_E = math.log2(math.e)
VMEM_LIMIT = 56 * 1024 * 1024

ROW_TILE = 512
CROSS_TILE = 1024
CROSS_SPLIT = 4
MEM_KV_SPLIT = 2
LRU_STEPS = 64
LRU_PITCH = LRU_STEPS + SUBLANES
ATT_TQ = 256
ATT_TK = 256
ATT_HEADS_PER_STEP = 8
FFN_TILE = 512
FFN_CHUNKS = ((0, 1024), (1024, 1024), (2048, 768))

NT_DIMS = (((1,), (1,)), ((), ()))


def _dot(a, b):
    return jnp.dot(a, b, preferred_element_type=F32)


def _dot_nt(a, b):
    return lax.dot_general(a, b, NT_DIMS, preferred_element_type=F32)


def _rms(x, g):
    ms = jnp.mean(x * x, axis=-1, keepdims=True)
    return x * lax.rsqrt(ms + EPS) * g


def _gelu_tanh(x):
    k = math.sqrt(2.0 / math.pi)
    half_x = 0.5 * x
    return half_x * jnp.tanh(x * (k + (k * 0.044715) * (x * x))) + half_x


def _split3(x):
    hi = x.astype(BF16).astype(F32)
    r = x - hi
    mid = r.astype(BF16).astype(F32)
    lo = (r - mid).astype(BF16).astype(F32)
    return hi, mid, lo


def _causal_conv(cur, prev_tail, w_ref, k_width):
    head_cur = cur[0:SUBLANES, :]
    row8 = lax.broadcasted_iota(jnp.int32, head_cur.shape, 0)
    main = cur * w_ref[k_width - 1:k_width, :]
    head = head_cur * w_ref[k_width - 1:k_width, :]
    for d in range(1, k_width):
        wk = w_ref[k_width - 1 - d:k_width - d, :]
        main = main + pltpu.roll(cur, d, 0) * wk
        shifted = jnp.where(row8 < d, pltpu.roll(prev_tail, d, 0), pltpu.roll(head_cur, d, 0))
        head = head + shifted * wk
    return main, head


def _even_in_kernel(x_ref, g_ref, w_ref, wt_ref, bf_ref, cw_ref, tri_ref,
                    pqt_ref, pk_ref, qt_ref, ka_ref, vt_ref, yb_ref, carry_c, carry_cu, *,
                    tiles_per_seq, offs):
    o_b, o_c, o_u, o_end = offs
    tm = x_ref.shape[0]
    fox_w = wt_ref.shape[0] // 2
    n_heads = fox_w // FOX_HEAD_DIM
    first = (pl.program_id(0) % tiles_per_seq) == 0

    @pl.when(first)
    def _():
        carry_c[...] = jnp.zeros_like(carry_c)
        carry_cu[...] = jnp.zeros_like(carry_cu)

    h = _rms(x_ref[...], g_ref[...]).astype(BF16)

    lane = lax.broadcasted_iota(jnp.int32, (tm, LANES), 1)
    kp = _dot(h, w_ref[:, 0:o_b])
    fl = pltpu.roll(kp[:, 0:LANES], LANES - F_LANE0, 1)
    fl = jnp.where(lane < 24, fl, 0.0) + bf_ref[...]

    lf = jax.nn.log_sigmoid(fl)
    hi, mid, lo = _split3(lf)
    parts = jnp.where(lane < 8, hi, jnp.where(lane < 16, mid, lo)).astype(BF16)
    r = _dot(tri_ref[...], parts)
    cum = r + pltpu.roll(r, LANES - 8, 1) + pltpu.roll(r, LANES - 16, 1)
    cum = jnp.where(lane < 8, cum + carry_c[...], 0.0)
    carry_c[...] = cum[tm - 1:tm, :]
    chi, cmid, clo = _split3(cum * LOG2_E)
    sel = jnp.where(lane < 8, chi,
                    jnp.where(lane < 16, pltpu.roll(cmid, 8, 1),
                              jnp.where(lane < 24, pltpu.roll(clo, 16, 1),
                                        jnp.where(lane == 24, 1.0, 0.0)))).astype(BF16)

    k0 = jnp.where(lane < FOX_HEAD_DIM, kp[:, 0:LANES], 0.0)
    kp = jnp.concatenate([k0, kp[:, LANES:]], axis=-1)
    ka_ref[...] = (kp + _dot(sel, pk_ref[...])).astype(BF16)

    scale = FOX_HEAD_DIM ** -0.5 * LOG2_E
    qt = _dot_nt(wt_ref[0:fox_w, :], h) * scale
    qb = _dot_nt(pqt_ref[...], sel)
    vt = _dot_nt(wt_ref[fox_w:, :], h)
    pad_rows = VT_SLOT - FOX_HEAD_DIM
    q_zero = jnp.zeros((HEAD_SLOT - FOX_HEAD_DIM - pad_rows, tm), F32)
    ones_blk = (lax.broadcasted_iota(jnp.int32, (pad_rows, tm), 0) == 0).astype(F32)
    q_pieces, v_pieces = [], []
    for n in range(n_heads):
        feat = slice(n * FOX_HEAD_DIM, (n + 1) * FOX_HEAD_DIM)
        q_pieces += [qt[feat, :], qb[n * pad_rows:(n + 1) * pad_rows, :], q_zero]
        v_pieces += [vt[feat, :], ones_blk]
    qt = jnp.concatenate(q_pieces, axis=0).astype(BF16)
    vt = jnp.concatenate(v_pieces, axis=0).astype(BF16)
    for c in range(qt_ref.shape[0]):
        w = qt_ref.shape[-1]
        qt_ref[c] = qt[:, c * w:(c + 1) * w]
    for c in range(vt_ref.shape[0]):
        w = vt_ref.shape[-1]
        vt_ref[c] = vt[:, c * w:(c + 1) * w]

    b_gate = _dot(h, w_ref[:, o_b:o_c])
    cu = _dot(h, w_ref[:, o_c:o_u]) * _dot(h, w_ref[:, o_u:o_end])
    main, head = _causal_conv(cu, carry_cu[...], cw_ref, SC_K)
    carry_cu[...] = cu[tm - SUBLANES:tm, :]
    yb_ref[...] = (b_gate * main).astype(BF16)
    yb_ref[0:SUBLANES, :] = (b_gate[0:SUBLANES, :] * head).astype(BF16)


def _fox_attn_kernel(qt_ref, ka_ref, vt_ref, o_ref, acc_ref):
    tq = qt_ref.shape[-1]
    qi = pl.program_id(2)
    krow = lax.broadcasted_iota(jnp.int32, (ATT_TK, tq), 0)
    qcol = lax.broadcasted_iota(jnp.int32, (ATT_TK, tq), 1)
    heads = range(acc_ref.shape[0])

    def step(j, ms, masked):
        k0 = pl.multiple_of(j * ATT_TK, ATT_TK)
        sts = [_dot(ka_ref[pl.ds(k0, ATT_TK), hh * HEAD_SLOT:(hh + 1) * HEAD_SLOT],
                    qt_ref[hh * HEAD_SLOT:(hh + 1) * HEAD_SLOT, :]) for hh in heads]
        ps, alphas, new = [], [], []
        for hh in heads:
            st = sts[hh]
            if masked:
                st = jnp.where(qcol >= krow, st, NEG_INF)
            m_new = jnp.maximum(ms[hh], jnp.max(st, axis=0, keepdims=True))
            alphas.append(jnp.exp2(ms[hh] - m_new))
            ps.append(jnp.exp2(st - m_new).astype(BF16))
            new.append(m_new)
        for hh in heads:
            vt = vt_ref[j, hh * VT_SLOT:(hh + 1) * VT_SLOT, :]
            acc_ref[hh] = alphas[hh] * acc_ref[hh] + _dot(vt, ps[hh])
        return tuple(new)

    acc_ref[...] = jnp.zeros_like(acc_ref)
    ms = tuple(jnp.full((1, tq), NEG_INF, F32) for _ in heads)
    full = functools.partial(step, masked=False)
    ms = lax.fori_loop(0, qi // 2, lambda jj, c: full(2 * jj + 1, full(2 * jj, c)), ms)

    @pl.when(qi % 2 == 1)
    def _():
        step(qi, full(qi - 1, ms), True)

    @pl.when(qi % 2 == 0)
    def _():
        step(qi, ms, True)

    o_t = jnp.concatenate(
        [acc_ref[hh, 0:FOX_HEAD_DIM, :] / acc_ref[hh, FOX_HEAD_DIM:FOX_HEAD_DIM + 1, :]
         for hh in heads], axis=0)
    o_ref[...] = o_t.T.astype(BF16)


def _lru_kernel(x_ref, g_ref, win_ref, cw_ref, cb_ref, wa_ref, ba_ref, wi_ref, bi_ref,
                lam_ref, y_ref, carry_h, carry_u, in_scr, out_scr):
    bsz, steps, width = x_ref.shape
    tm = steps * bsz
    tail = (RG_CONV_K - 1) * bsz
    lane_tiles = range(width // LANES)

    @pl.when(pl.program_id(0) == 0)
    def _():
        carry_h[...] = jnp.zeros_like(carry_h)
        carry_u[...] = jnp.zeros_like(carry_u)

    for b in range(bsz):
        hb = _rms(x_ref[b], g_ref[...])
        for k in lane_tiles:
            in_scr[k, b * LRU_PITCH:b * LRU_PITCH + steps, :] = hb[:, k * LANES:(k + 1) * LANES]
    h = jnp.concatenate(
        [jnp.concatenate([in_scr[k, pl.ds(t, bsz, stride=LRU_PITCH), :] for k in lane_tiles],
                         axis=-1) for t in range(steps)], axis=0).astype(BF16)
    u = _dot(h, win_ref[:, width:2 * width])

    ext = jnp.concatenate([carry_u[...], u], axis=0)
    carry_u[...] = u[tm - tail:tm, :]
    uc = cb_ref[...] + u * cw_ref[RG_CONV_K - 1:RG_CONV_K, :]
    for d in range(1, RG_CONV_K):
        shifted = ext[tail - d * bsz:tail - d * bsz + tm, :]
        uc = uc + shifted * cw_ref[RG_CONV_K - 1 - d:RG_CONV_K - d, :]
    ucb = uc.astype(BF16)

    nb = width // LRU_BW
    tr = jnp.tanh(jnp.concatenate(
        [_dot(ucb[:, n * LRU_BW:(n + 1) * LRU_BW], wa_ref[n]) for n in range(nb)], axis=-1)
        + ba_ref[...])
    ti = jnp.tanh(jnp.concatenate(
        [_dot(ucb[:, n * LRU_BW:(n + 1) * LRU_BW], wi_ref[n]) for n in range(nb)], axis=-1)
        + bi_ref[...])

    half_c = (0.5 * RG_C) * jax.nn.softplus(-lam_ref[...])
    neg_log_a = tr * half_c + half_c
    a = jnp.exp2(neg_log_a * (-LOG2_E))
    v = jnp.tanh(neg_log_a) * (1.0 + a * a)
    root = jnp.where(v > 0.0, v * lax.rsqrt(v), 0.0)
    b = root * ((ti * 0.5 + 0.5) * uc)

    state = carry_h[...]
    hs = []
    for t in range(steps):
        state = a[t * bsz:(t + 1) * bsz, :] * state + b[t * bsz:(t + 1) * bsz, :]
        hs.append(state)
    carry_h[...] = state
    hs = jnp.concatenate(hs, axis=0)

    y = _gelu_tanh(_dot(h, win_ref[:, 0:width])) * hs
    for t in range(steps):
        for k in lane_tiles:
            out_scr[k, pl.ds(t, bsz, stride=LRU_PITCH), :] = (
                y[t * bsz:(t + 1) * bsz, k * LANES:(k + 1) * LANES])
    for b in range(bsz):
        y_ref[b] = jnp.concatenate(
            [out_scr[k, b * LRU_PITCH:b * LRU_PITCH + steps, :] for k in lane_tiles],
            axis=-1).astype(BF16)


def _mem_kv_kernel(mem_ref, g_ref, w_ref, kt_ref, v_ref):
    nb, mlen, d_model = mem_ref.shape
    m = _rms(mem_ref[...].reshape(nb * mlen, d_model), g_ref[...]).astype(BF16)
    k = _dot(m, w_ref[:, 0:d_model])
    v = _dot(m, w_ref[:, d_model:2 * d_model]).astype(BF16)
    for b in range(nb):
        kt_ref[b] = k[b * mlen:(b + 1) * mlen, :].T.astype(BF16)
        v_ref[b] = v[b * mlen:(b + 1) * mlen, :]


def _cross_kernel(*refs, n_parts):
    y_refs = refs[0:n_parts]
    w_refs = refs[n_parts:2 * n_parts]
    (x_ref, gpost_ref, gcpre_ref, wq_ref, kt_ref, v_ref, wo_ref, gcpost_ref, gfpre_ref,
     wgu_ref, wd_ref, xo_ref, ho_ref, wgu_out_ref, wd_out_ref) = refs[2 * n_parts:]
    wgu_out_ref[...] = wgu_ref[...].astype(BF16)
    wd_out_ref[...] = wd_ref[...].astype(BF16)
    tm, d_model = x_ref.shape
    hd = d_model // MEM_HEADS
    sub = tm // CROSS_SPLIT
    rows = [pl.ds(s * sub, sub) for s in range(CROSS_SPLIT)]
    each = lambda f, *lists: [f(*args) for args in zip(*lists)]

    def mix_out(r):
        y = _dot(y_refs[0][r, :], w_refs[0][...])
        for p in range(1, n_parts):
            y = y + _dot(y_refs[p][r, :], w_refs[p][...])
        return y

    ys = each(mix_out, rows)
    x1s = each(lambda r, y: x_ref[r, :] + _rms(y, gpost_ref[...]), rows, ys)
    hs = each(lambda x1: _rms(x1, gcpre_ref[...]).astype(BF16), x1s)
    qs = each(lambda h: (_dot(h, wq_ref[...]) * (hd ** -0.5)).astype(BF16), hs)
    outs = [[] for _ in rows]
    for n in range(MEM_HEADS):
        ss = each(lambda q: _dot(q[:, n * hd:(n + 1) * hd], kt_ref[n * hd:(n + 1) * hd, :]), qs)
        es = each(lambda s: jnp.exp(s - jnp.max(s, axis=-1, keepdims=True)), ss)
        os_ = each(lambda e: _dot(e.astype(BF16), v_ref[:, n * hd:(n + 1) * hd])
                   / jnp.sum(e, axis=-1, keepdims=True), es)
        for lst, o in zip(outs, os_):
            lst.append(o.astype(BF16))
    y2s = each(lambda o: _dot(jnp.concatenate(o, axis=-1), wo_ref[...]), outs)
    x2s = each(lambda x1, y2: x1 + _rms(y2, gcpost_ref[...]), x1s, y2s)
    for r, x2 in zip(rows, x2s):
        xo_ref[r, :] = x2
        ho_ref[r, :] = _rms(x2, gfpre_ref[...]).astype(BF16)


def _ffn_kernel(h_ref, x_ref, wg_ref, wu_ref, wd_ref, g_ref, xo_ref):
    h = h_ref[...]
    acc = None
    for c0, cw in FFN_CHUNKS:
        a = (jax.nn.silu(_dot(h, wg_ref[:, c0:c0 + cw])) * _dot(h, wu_ref[:, c0:c0 + cw]))
        part = _dot(a.astype(BF16), wd_ref[c0:c0 + cw, :])
        acc = part if acc is None else acc + part
    xo_ref[...] = x_ref[...] + _rms(acc, g_ref[...])


def _params(sem):
    return pltpu.CompilerParams(dimension_semantics=sem, vmem_limit_bytes=VMEM_LIMIT)


def _const_spec(shape, index):
    return pl.BlockSpec(shape, index, pipeline_mode=pl.Buffered(1))


def _layer_vec_spec(row, width):
    return _const_spec((None, 1, width), lambda *_: (row, 0, 0))


def _even_in(x2, vecs, r_pre, w_all, w_t, bf_pad, conv_w, e, tri, pqt, pk, seq, offs):
    n, d_model = x2.shape
    tm = ROW_TILE
    sc_w = offs[2] - offs[1]
    q_rows = pk.shape[1]
    n_heads = q_rows // HEAD_SLOT
    vt_rows = n_heads * VT_SLOT
    tiles_per_seq = seq // tm
    row = lambda i: (i, 0)
    chunked = lambda i: (i // tiles_per_seq, i % tiles_per_seq, 0, 0)
    kern = functools.partial(_even_in_kernel, tiles_per_seq=tiles_per_seq, offs=offs)
    return pl.pallas_call(
        kern,
        grid=(n // tm,),
        in_specs=[
            pl.BlockSpec((tm, d_model), row),
            _layer_vec_spec(r_pre, d_model),
            _const_spec((None, d_model, offs[-1]), lambda i: (e, 0, 0)),
            _const_spec((None, w_t.shape[1], d_model), lambda i: (e, 0, 0)),
            _const_spec((None, 1, LANES), lambda i: (e, 0, 0)),
            _const_spec((None, SC_K, sc_w), lambda i: (e, 0, 0)),
            _const_spec((tm, tm), lambda i: (0, 0)),
            _const_spec(pqt.shape, lambda i: (0, 0)),
            _const_spec((LANES, q_rows), lambda i: (0, 0)),
        ],
        out_specs=[
            pl.BlockSpec((None, tm // ATT_TQ, q_rows, ATT_TQ), chunked),
            pl.BlockSpec((tm, q_rows), row),
            pl.BlockSpec((None, tm // ATT_TK, vt_rows, ATT_TK), chunked),
            pl.BlockSpec((tm, sc_w), row),
        ],
        out_shape=[
            jax.ShapeDtypeStruct((n // seq, seq // ATT_TQ, q_rows, ATT_TQ), BF16),
            jax.ShapeDtypeStruct((n, q_rows), BF16),
            jax.ShapeDtypeStruct((n // seq, seq // ATT_TK, vt_rows, ATT_TK), BF16),
            jax.ShapeDtypeStruct((n, sc_w), BF16),
        ],
        scratch_shapes=[pltpu.VMEM((1, LANES), F32), pltpu.VMEM((SUBLANES, sc_w), F32)],
        compiler_params=_params(("arbitrary",)),
        name="even_in",
    )(x2, vecs, w_all, w_t, bf_pad, conv_w, tri, pqt, pk)


def _fox_attn(qt, ka, vt, seq):
    bsz, nq, _, _ = qt.shape
    _, nkb, vt_rows, _ = vt.shape
    hps = ATT_HEADS_PER_STEP
    n_groups = vt_rows // (hps * VT_SLOT)
    return pl.pallas_call(
        _fox_attn_kernel,
        grid=(bsz, n_groups, nq),
        in_specs=[
            pl.BlockSpec((None, None, hps * HEAD_SLOT, ATT_TQ), lambda b, p, i: (b, i, p, 0)),
            pl.BlockSpec((seq, hps * HEAD_SLOT), lambda b, p, i: (b, p)),
            pl.BlockSpec((None, nkb, hps * VT_SLOT, ATT_TK), lambda b, p, i: (b, 0, p, 0)),
        ],
        out_specs=pl.BlockSpec((ATT_TQ, hps * FOX_HEAD_DIM), lambda b, p, i: (b * nq + i, p)),
        out_shape=jax.ShapeDtypeStruct((bsz * seq, n_groups * hps * FOX_HEAD_DIM), BF16),
        scratch_shapes=[pltpu.VMEM((hps, VT_SLOT, ATT_TQ), F32)],
        compiler_params=_params(("arbitrary", "arbitrary", "arbitrary")),
        name="fox_attn",
    )(qt, ka, vt)


def _lru(x2, vecs, rows, w_in, conv_w, w_a, w_i, o, seq):
    r_pre, r_cb, r_ba, r_bi, r_lam = rows
    n, d_model = x2.shape
    bsz = n // seq
    assert bsz == SUBLANES
    steps = LRU_STEPS
    nb = d_model // LRU_BW
    blk = pl.BlockSpec((bsz, steps, d_model), lambda i: (0, i, 0))
    relayout = pltpu.VMEM((d_model // LANES, bsz * LRU_PITCH, LANES), F32)
    y = pl.pallas_call(
        _lru_kernel,
        grid=(seq // steps,),
        in_specs=[
            blk,
            _layer_vec_spec(r_pre, d_model),
            _const_spec((None, d_model, 2 * d_model), lambda i: (o, 0, 0)),
            _const_spec((None, RG_CONV_K, d_model), lambda i: (o, 0, 0)),
            _layer_vec_spec(r_cb, d_model),
            _const_spec((None, nb, LRU_BW, LRU_BW), lambda i: (o, 0, 0, 0)),
            _layer_vec_spec(r_ba, d_model),
            _const_spec((None, nb, LRU_BW, LRU_BW), lambda i: (o, 0, 0, 0)),
            _layer_vec_spec(r_bi, d_model),
            _layer_vec_spec(r_lam, d_model),
        ],
        out_specs=blk,
        out_shape=jax.ShapeDtypeStruct((bsz, seq, d_model), BF16),
        scratch_shapes=[pltpu.VMEM((bsz, d_model), F32),
                        pltpu.VMEM(((RG_CONV_K - 1) * bsz, d_model), F32),
                        relayout, relayout],
        compiler_params=_params(("arbitrary",)),
        name="rglru",
    )(x2.reshape(bsz, seq, d_model), vecs, w_in, conv_w, vecs, w_a, vecs, w_i, vecs, vecs)
    return y.reshape(n, d_model)


def _mem_kv(mem, vecs, row0, w_xkv):
    bsz, mlen, d_model = mem.shape
    depth = w_xkv.shape[0]
    nb = bsz // MEM_KV_SPLIT
    return pl.pallas_call(
        _mem_kv_kernel,
        grid=(depth, MEM_KV_SPLIT),
        in_specs=[
            pl.BlockSpec((nb, mlen, d_model), lambda l, b: (b, 0, 0)),
            pl.BlockSpec((None, 1, d_model), lambda l, b: (row0 + l, 0, 0)),
            pl.BlockSpec((None, d_model, 2 * d_model), lambda l, b: (l, 0, 0)),
        ],
        out_specs=[
            pl.BlockSpec((None, nb, d_model, mlen), lambda l, b: (l, b, 0, 0)),
            pl.BlockSpec((None, nb, mlen, d_model), lambda l, b: (l, b, 0, 0)),
        ],
        out_shape=[jax.ShapeDtypeStruct((depth, bsz, d_model, mlen), BF16),
                   jax.ShapeDtypeStruct((depth, bsz, mlen, d_model), BF16)],
        compiler_params=_params(("arbitrary", "arbitrary")),
        name="mem_kv",
    )(mem, vecs, w_xkv)


def _cross(y_parts, w_parts, x2, layer, vecs, rows, w_xq, k_mem, v_mem, w_xo, w_gu, w_down,
           seq):
    n, d_model = x2.shape
    d_ff = w_down.shape[1]
    steps = n // CROSS_TILE
    gu_rows, down_rows = d_model // steps, d_ff // steps
    slab = lambda i: (i, 0)
    r_post, r_cpre, r_cpost, r_fpre = rows
    tm = CROSS_TILE
    tiles_per_seq = seq // tm
    mlen = v_mem.shape[2]
    row = lambda i: (i, 0)
    in_specs = [pl.BlockSpec((tm, yp.shape[1]), row) for yp in y_parts]
    w_arrays = []
    for yp, (w_arr, idx, blk) in zip(y_parts, w_parts):
        in_specs.append(_const_spec((None, yp.shape[1], d_model),
                                    lambda i, idx=idx, blk=blk: (idx, blk, 0)))
        w_arrays.append(w_arr)
    mem_index = lambda i: (layer, i // tiles_per_seq, 0, 0)
    in_specs += [
        pl.BlockSpec((tm, d_model), row),
        _layer_vec_spec(r_post, d_model),
        _layer_vec_spec(r_cpre, d_model),
        _const_spec((None, d_model, d_model), lambda i: (layer, 0, 0)),
        pl.BlockSpec((None, None, d_model, mlen), mem_index),
        pl.BlockSpec((None, None, mlen, d_model), mem_index),
        _const_spec((None, d_model, d_model), lambda i: (layer, 0, 0)),
        _layer_vec_spec(r_cpost, d_model),
        _layer_vec_spec(r_fpre, d_model),
        pl.BlockSpec((None, gu_rows, 2 * d_ff), lambda i: (layer, i, 0)),
        pl.BlockSpec((None, down_rows, d_model), lambda i: (layer, i, 0)),
    ]
    kern = functools.partial(_cross_kernel, n_parts=len(y_parts))
    return pl.pallas_call(
        kern,
        grid=(steps,),
        in_specs=in_specs,
        out_specs=[pl.BlockSpec((tm, d_model), row), pl.BlockSpec((tm, d_model), row),
                   pl.BlockSpec((gu_rows, 2 * d_ff), slab),
                   pl.BlockSpec((down_rows, d_model), slab)],
        out_shape=[jax.ShapeDtypeStruct((n, d_model), F32),
                   jax.ShapeDtypeStruct((n, d_model), BF16),
                   jax.ShapeDtypeStruct((d_model, 2 * d_ff), BF16),
                   jax.ShapeDtypeStruct((d_ff, d_model), BF16)],
        compiler_params=_params(("arbitrary",)),
        name="cross",
    )(*y_parts, *w_arrays, x2, vecs, vecs, w_xq, k_mem, v_mem, w_xo, vecs, vecs, w_gu, w_down)


def _ffn(h, x2, w_gu, w_down, vecs, r_post):
    n, d_model = x2.shape
    d_ff = w_down.shape[0]
    tm = FFN_TILE
    row = lambda i: (i, 0)
    return pl.pallas_call(
        _ffn_kernel,
        grid=(n // tm,),
        in_specs=[
            pl.BlockSpec((tm, d_model), row),
            pl.BlockSpec((tm, d_model), row),
            _const_spec((d_model, d_ff), lambda i: (0, 0)),
            _const_spec((d_model, d_ff), lambda i: (0, 1)),
            _const_spec((d_ff, d_model), lambda i: (0, 0)),
            _layer_vec_spec(r_post, d_model),
        ],
        out_specs=pl.BlockSpec((tm, d_model), row),
        out_shape=jax.ShapeDtypeStruct((n, d_model), F32),
        compiler_params=_params(("arbitrary",)),
        name="ffn",
    )(h, x2, w_gu, w_gu, w_down, vecs)


def _pad_heads(w, n_heads, slot):
    e, d, _ = w.shape
    w = w.reshape(e, d, n_heads, FOX_HEAD_DIM)
    w = jnp.pad(w, ((0, 0), (0, 0), (0, 0), (0, slot - FOX_HEAD_DIM)))
    return w.reshape(e, d, n_heads * slot)


def _bias_placement(n_heads):
    pad_rows = VT_SLOT - FOX_HEAD_DIM
    pqt = [[0.0] * LANES for _ in range(n_heads * pad_rows)]
    pk = [[0.0] * (n_heads * HEAD_SLOT) for _ in range(LANES)]
    one_lane = 24
    for h in range(n_heads):
        base = h * HEAD_SLOT + FOX_HEAD_DIM
        for part in range(3):
            pqt[h * pad_rows + part][part * 8 + h] = 1.0
            pk[one_lane][base + part] = 1.0
            pqt[h * pad_rows + 3 + part][one_lane] = 1.0
            pk[part * 8 + h][base + 3 + part] = -1.0
    return jnp.array(pqt, BF16), jnp.array(pk, BF16)


def kernel(x, mem, g_mix_pre, g_mix_post, g_cross_pre, g_mem, g_cross_post, g_ffn_pre, g_ffn_post, w_xq, w_xkv, w_xo, w_ffn_gu, w_ffn_down, ab_w_in, ab_b_f, ab_conv_w, ab_w_out, c_w_in, c_conv_w, c_conv_b, c_w_a, c_b_a, c_w_i, c_b_i, c_lam, c_w_out):
    bsz, seq, d_model = x.shape
    depth = g_mix_pre.shape[0]
    n = bsz * seq
    n_heads = ab_b_f.shape[1]
    fox_w = n_heads * FOX_HEAD_DIM
    sc_w = ab_conv_w.shape[2]
    assert n_heads <= 8 and seq % ROW_TILE == 0 and seq % LRU_STEPS == 0 and seq % ATT_TQ == 0

    n_odd = c_lam.shape[0]
    gains = (g_mix_pre, g_mix_post, g_cross_pre, g_mem, g_cross_post, g_ffn_pre, g_ffn_post)
    odd_vecs = (c_conv_b, 0.5 * c_b_a.reshape(n_odd, -1), 0.5 * c_b_i.reshape(n_odd, -1), c_lam)
    vecs = jnp.concatenate(gains + odd_vecs, axis=0)[:, None, :]
    (R_MIX_PRE, R_MIX_POST, R_CROSS_PRE, R_MEM, R_CROSS_POST, R_FFN_PRE,
     R_FFN_POST) = (k * depth for k in range(len(gains)))
    R_CONV_B, R_B_A, R_B_I, R_LAM = (len(gains) * depth + k * n_odd for k in range(4))

    i1, i2, i3 = fox_w, 2 * fox_w, 3 * fox_w
    i4 = i3 + n_heads
    i5, i6 = i4 + sc_w, i4 + 2 * sc_w
    w_in_b = ab_w_in.astype(BF16)
    w_f = w_in_b[:, :, i3:i4]
    w_k = _pad_heads(w_in_b[:, :, i1:i2], n_heads, HEAD_SLOT)
    w_k = w_k.at[:, :, F_LANE0:F_LANE0 + 3 * n_heads].set(jnp.concatenate([w_f, w_f, w_f], -1))
    pieces = [w_k, w_in_b[:, :, i4:i5], w_in_b[:, :, i5:i6], w_in_b[:, :, i6:]]
    w_t = jnp.concatenate([w_in_b[:, :, 0:i1], w_in_b[:, :, i2:i3]], axis=-1)
    w_t = w_t.transpose(0, 2, 1)
    offs = []
    for p in pieces:
        offs.append((offs[-1] if offs else 0) + p.shape[-1])
    offs = tuple(offs)
    w_even = jnp.concatenate(pieces, axis=-1)
    b_f3 = jnp.pad(jnp.concatenate([ab_b_f, ab_b_f, ab_b_f], axis=-1),
                   ((0, 0), (0, LANES - 3 * n_heads))).reshape(-1, 1, LANES)
    tri = (lax.broadcasted_iota(jnp.int32, (ROW_TILE, ROW_TILE), 0)
           >= lax.broadcasted_iota(jnp.int32, (ROW_TILE, ROW_TILE), 1)).astype(BF16)
    pqt, pk = _bias_placement(n_heads)

    w_out_even = ab_w_out.astype(BF16)
    w_c_in = c_w_in.astype(BF16)
    w_c_out = c_w_out.astype(BF16)
    w_a = (0.5 * c_w_a).astype(BF16)
    w_i = (0.5 * c_w_i).astype(BF16)
    w_xq_b = w_xq.astype(BF16)
    w_xo_b = w_xo.astype(BF16)

    k_mem, v_mem = _mem_kv(mem, vecs, R_MEM, w_xkv.astype(BF16))

    x2 = x.reshape(n, d_model)
    for layer in range(depth):
        if layer % 2 == 0:
            e = layer // 2
            qt, ka, vt, yb = _even_in(x2, vecs, R_MIX_PRE + layer, w_even, w_t, b_f3,
                                      ab_conv_w, e, tri, pqt, pk, seq, offs)
            ya = _fox_attn(qt, ka, vt, seq)
            y_parts = [ya, yb]
            assert fox_w == sc_w
            w_parts = [(w_out_even, e, 0), (w_out_even, e, 1)]
        else:
            o = layer // 2
            rows = (R_MIX_PRE + layer, R_CONV_B + o, R_B_A + o, R_B_I + o, R_LAM + o)
            ym = _lru(x2, vecs, rows, w_c_in, c_conv_w, w_a, w_i, o, seq)
            y_parts = [ym]
            w_parts = [(w_c_out, o, 0)]
        rows = (R_MIX_POST + layer, R_CROSS_PRE + layer, R_CROSS_POST + layer,
                R_FFN_PRE + layer)
        x2, h, w_gu_b, w_down_b = _cross(y_parts, w_parts, x2, layer, vecs, rows, w_xq_b,
                                         k_mem, v_mem, w_xo_b, w_ffn_gu, w_ffn_down, seq)
        x2 = _ffn(h, x2, w_gu_b, w_down_b, vecs, R_FFN_POST + layer)
    return x2.reshape(bsz, seq, d_model)
```

```python
import functools
import math

import jax
import jax.numpy as jnp
from jax import lax
from jax.experimental import pallas as pl
from jax.experimental.pallas import tpu as pltpu

F32 = jnp.float32
BF16 = jnp.bfloat16

EPS = 1e-6
NEG_INF = -1e30
FOX_HEAD_DIM = 64
SC_K = 3
LRU_BW = 256
RG_CONV_K = 4
RG_C = 8.0
MEM_HEADS = 4

LANES = 128
SUBLANES = 8
HEAD_SLOT = LANES
VT_SLOT = 80
F_LANE0 = FOX_HEAD_DIM + 6
LOG2_E = math.log2(math.e)
VMEM_LIMIT = 56 * 1024 * 1024

ROW_TILE = 512
CROSS_TILE = 1024
CROSS_SPLIT = 4
MEM_KV_SPLIT = 2
LRU_STEPS = 64
LRU_PITCH = LRU_STEPS + SUBLANES
ATT_TQ = 256
ATT_TK = 256
ATT_HEADS_PER_STEP = 8
FFN_TILE = 512
FFN_CHUNKS = ((0, 1024), (1024, 1024), (2048, 768))

NT_DIMS = (((1,), (1,)), ((), ()))


def _dot(a, b):
    return jnp.dot(a, b, preferred_element_type=F32)


def _dot_nt(a, b):
    return lax.dot_general(a, b, NT_DIMS, preferred_element_type=F32)


def _rms(x, g):
    ms = jnp.mean(x * x, axis=-1, keepdims=True)
    return x * lax.rsqrt(ms + EPS) * g


def _gelu_tanh(x):
    k = math.sqrt(2.0 / math.pi)
    half_x = 0.5 * x
    return half_x * jnp.tanh(x * (k + (k * 0.044715) * (x * x))) + half_x


def _split3(x):
    hi = x.astype(BF16).astype(F32)
    r = x - hi
    mid = r.astype(BF16).astype(F32)
    lo = (r - mid).astype(BF16).astype(F32)
    return hi, mid, lo


def _causal_conv(cur, prev_tail, w_ref, k_width):
    head_cur = cur[0:SUBLANES, :]
    row8 = lax.broadcasted_iota(jnp.int32, head_cur.shape, 0)
    main = cur * w_ref[k_width - 1:k_width, :]
    head = head_cur * w_ref[k_width - 1:k_width, :]
    for d in range(1, k_width):
        wk = w_ref[k_width - 1 - d:k_width - d, :]
        main = main + pltpu.roll(cur, d, 0) * wk
        shifted = jnp.where(row8 < d, pltpu.roll(prev_tail, d, 0), pltpu.roll(head_cur, d, 0))
        head = head + shifted * wk
    return main, head


def _even_in_kernel(x_ref, g_ref, w_ref, wt_ref, bf_ref, cw_ref, tri_ref,
                    pqt_ref, pk_ref, qt_ref, ka_ref, vt_ref, yb_ref, carry_c, carry_cu, *,
                    tiles_per_seq, offs):
    o_b, o_c, o_u, o_end = offs
    tm = x_ref.shape[0]
    fox_w = wt_ref.shape[0] // 2
    n_heads = fox_w // FOX_HEAD_DIM
    first = (pl.program_id(0) % tiles_per_seq) == 0

    @pl.when(first)
    def _():
        carry_c[...] = jnp.zeros_like(carry_c)
        carry_cu[...] = jnp.zeros_like(carry_cu)

    h = _rms(x_ref[...], g_ref[...]).astype(BF16)

    lane = lax.broadcasted_iota(jnp.int32, (tm, LANES), 1)
    kp = _dot(h, w_ref[:, 0:o_b])
    scale = FOX_HEAD_DIM ** -0.5 * LOG2_E
    qt = _dot_nt(wt_ref[0:fox_w, :], h) * scale

    fl = pltpu.roll(kp[:, 0:LANES], LANES - F_LANE0, 1)
    fl = jnp.where(lane < 24, fl, 0.0) + bf_ref[...]
    lf = jax.nn.log_sigmoid(fl)
    hi, mid, lo = _split3(lf)
    parts = jnp.where(lane < 8, hi, jnp.where(lane < 16, mid, lo)).astype(BF16)
    vt = _dot_nt(wt_ref[fox_w:, :], h)
    r = _dot(tri_ref[...], parts)
    b_gate = _dot(h, w_ref[:, o_b:o_c])
    cu = _dot(h, w_ref[:, o_c:o_u]) * _dot(h, w_ref[:, o_u:o_end])

    cum = r + pltpu.roll(r, LANES - 8, 1) + pltpu.roll(r, LANES - 16, 1)
    cum = jnp.where(lane < 8, cum + carry_c[...], 0.0)
    carry_c[...] = cum[tm - 1:tm, :]
    chi, cmid, clo = _split3(cum * LOG2_E)
    sel = jnp.where(lane < 8, chi,
                    jnp.where(lane < 16, pltpu.roll(cmid, 8, 1),
                              jnp.where(lane < 24, pltpu.roll(clo, 16, 1),
                                        jnp.where(lane == 24, 1.0, 0.0)))).astype(BF16)

    k0 = jnp.where(lane < FOX_HEAD_DIM, kp[:, 0:LANES], 0.0)
    kp = jnp.concatenate([k0, kp[:, LANES:]], axis=-1)
    ka_ref[...] = (kp + _dot(sel, pk_ref[...])).astype(BF16)

    qb = _dot_nt(pqt_ref[...], sel)
    pad_rows = VT_SLOT - FOX_HEAD_DIM
    q_zero = jnp.zeros((HEAD_SLOT - FOX_HEAD_DIM - pad_rows, tm), F32)
    ones_blk = (lax.broadcasted_iota(jnp.int32, (pad_rows, tm), 0) == 0).astype(F32)
    q_pieces, v_pieces = [], []
    for n in range(n_heads):
        feat = slice(n * FOX_HEAD_DIM, (n + 1) * FOX_HEAD_DIM)
        q_pieces += [qt[feat, :], qb[n * pad_rows:(n + 1) * pad_rows, :], q_zero]
        v_pieces += [vt[feat, :], ones_blk]
    qt = jnp.concatenate(q_pieces, axis=0).astype(BF16)
    vt = jnp.concatenate(v_pieces, axis=0).astype(BF16)
    for c in range(qt_ref.shape[0]):
        w = qt_ref.shape[-1]
        qt_ref[c] = qt[:, c * w:(c + 1) * w]
    for c in range(vt_ref.shape[0]):
        w = vt_ref.shape[-1]
        vt_ref[c] = vt[:, c * w:(c + 1) * w]

    main, head = _causal_conv(cu, carry_cu[...], cw_ref, SC_K)
    carry_cu[...] = cu[tm - SUBLANES:tm, :]
    yb_ref[...] = (b_gate * main).astype(BF16)
    yb_ref[0:SUBLANES, :] = (b_gate[0:SUBLANES, :] * head).astype(BF16)


def _fox_attn_kernel(qt_ref, ka_ref, vt_ref, o_ref, acc_ref):
    tq = qt_ref.shape[-1]
    qi = pl.program_id(2)
    krow = lax.broadcasted_iota(jnp.int32, (ATT_TK, tq), 0)
    qcol = lax.broadcasted_iota(jnp.int32, (ATT_TK, tq), 1)
    heads = range(acc_ref.shape[0])

    def scores(j):
        k0 = pl.multiple_of(j * ATT_TK, ATT_TK)
        return [_dot(ka_ref[pl.ds(k0, ATT_TK), hh * HEAD_SLOT:(hh + 1) * HEAD_SLOT],
                     qt_ref[hh * HEAD_SLOT:(hh + 1) * HEAD_SLOT, :]) for hh in heads]

    def update(j, sts, ms, masked):
        ps, alphas, new = [], [], []
        for hh in heads:
            st = sts[hh]
            if masked:
                st = jnp.where(qcol >= krow, st, NEG_INF)
            m_new = jnp.maximum(ms[hh], jnp.max(st, axis=0, keepdims=True))
            alphas.append(jnp.exp2(ms[hh] - m_new))
            ps.append(jnp.exp2(st - m_new).astype(BF16))
            new.append(m_new)
        for hh in heads:
            vt = vt_ref[j, hh * VT_SLOT:(hh + 1) * VT_SLOT, :]
            acc_ref[hh] = alphas[hh] * acc_ref[hh] + _dot(vt, ps[hh])
        return tuple(new)

    def pair(ja, jb, ms, masked_b):
        sts_a, sts_b = scores(ja), scores(jb)
        return update(jb, sts_b, update(ja, sts_a, ms, False), masked_b)

    acc_ref[...] = jnp.zeros_like(acc_ref)
    ms = tuple(jnp.full((1, tq), NEG_INF, F32) for _ in heads)
    ms = lax.fori_loop(0, qi // 2, lambda jj, c: pair(2 * jj, 2 * jj + 1, c, False), ms)

    @pl.when(qi % 2 == 1)
    def _():
        pair(qi - 1, qi, ms, True)

    @pl.when(qi % 2 == 0)
    def _():
        update(qi, scores(qi), ms, True)

    o_t = jnp.concatenate(
        [acc_ref[hh, 0:FOX_HEAD_DIM, :] / acc_ref[hh, FOX_HEAD_DIM:FOX_HEAD_DIM + 1, :]
         for hh in heads], axis=0)
    o_ref[...] = o_t.T.astype(BF16)


def _lru_kernel(x_ref, g_ref, win_ref, cw_ref, cb_ref, wa_ref, ba_ref, wi_ref, bi_ref,
                lam_ref, y_ref, carry_h, carry_u, in_scr, out_scr):
    bsz, steps, width = x_ref.shape
    tm = steps * bsz
    tail = (RG_CONV_K - 1) * bsz
    lane_tiles = range(width // LANES)

    @pl.when(pl.program_id(0) == 0)
    def _():
        carry_h[...] = jnp.zeros_like(carry_h)
        carry_u[...] = jnp.zeros_like(carry_u)

    for b in range(bsz):
        hb = _rms(x_ref[b], g_ref[...])
        for k in lane_tiles:
            in_scr[k, b * LRU_PITCH:b * LRU_PITCH + steps, :] = hb[:, k * LANES:(k + 1) * LANES]
    h = jnp.concatenate(
        [jnp.concatenate([in_scr[k, pl.ds(t, bsz, stride=LRU_PITCH), :] for k in lane_tiles],
                         axis=-1) for t in range(steps)], axis=0).astype(BF16)
    u = _dot(h, win_ref[:, width:2 * width])

    ext = jnp.concatenate([carry_u[...], u], axis=0)
    carry_u[...] = u[tm - tail:tm, :]
    uc = cb_ref[...] + u * cw_ref[RG_CONV_K - 1:RG_CONV_K, :]
    for d in range(1, RG_CONV_K):
        shifted = ext[tail - d * bsz:tail - d * bsz + tm, :]
        uc = uc + shifted * cw_ref[RG_CONV_K - 1 - d:RG_CONV_K - d, :]
    ucb = uc.astype(BF16)

    nb = width // LRU_BW
    tr = jnp.tanh(jnp.concatenate(
        [_dot(ucb[:, n * LRU_BW:(n + 1) * LRU_BW], wa_ref[n]) for n in range(nb)], axis=-1)
        + ba_ref[...])
    ti = jnp.tanh(jnp.concatenate(
        [_dot(ucb[:, n * LRU_BW:(n + 1) * LRU_BW], wi_ref[n]) for n in range(nb)], axis=-1)
        + bi_ref[...])

    half_c = (0.5 * RG_C) * jax.nn.softplus(-lam_ref[...])
    neg_log_a = tr * half_c + half_c
    a = jnp.exp2(neg_log_a * (-LOG2_E))
    v = jnp.tanh(neg_log_a) * (1.0 + a * a)
    root = jnp.where(v > 0.0, v * lax.rsqrt(v), 0.0)
    b = root * ((ti * 0.5 + 0.5) * uc)

    state = carry_h[...]
    hs = []
    for t in range(steps):
        state = a[t * bsz:(t + 1) * bsz, :] * state + b[t * bsz:(t + 1) * bsz, :]
        hs.append(state)
    carry_h[...] = state
    hs = jnp.concatenate(hs, axis=0)

    y = _gelu_tanh(_dot(h, win_ref[:, 0:width])) * hs
    for t in range(steps):
        for k in lane_tiles:
            out_scr[k, pl.ds(t, bsz, stride=LRU_PITCH), :] = (
                y[t * bsz:(t + 1) * bsz, k * LANES:(k + 1) * LANES])
    for b in range(bsz):
        y_ref[b] = jnp.concatenate(
            [out_scr[k, b * LRU_PITCH:b * LRU_PITCH + steps, :] for k in lane_tiles],
            axis=-1).astype(BF16)


def _mem_kv_kernel(mem_ref, g_ref, w_ref, kt_ref, v_ref):
    nb, mlen, d_model = mem_ref.shape
    m = _rms(mem_ref[...].reshape(nb * mlen, d_model), g_ref[...]).astype(BF16)
    k = _dot(m, w_ref[:, 0:d_model])
    v = _dot(m, w_ref[:, d_model:2 * d_model]).astype(BF16)
    for b in range(nb):
        kt_ref[b] = k[b * mlen:(b + 1) * mlen, :].T.astype(BF16)
        v_ref[b] = v[b * mlen:(b + 1) * mlen, :]


def _cross_kernel(*refs, n_parts):
    y_refs = refs[0:n_parts]
    w_refs = refs[n_parts:2 * n_parts]
    (x_ref, gpost_ref, gcpre_ref, wq_ref, kt_ref, v_ref, wo_ref, gcpost_ref,
     wgu_ref, wd_ref, xo_ref, wgu_out_ref, wd_out_ref) = refs[2 * n_parts:]
    wgu_out_ref[...] = wgu_ref[...].astype(BF16)
    wd_out_ref[...] = wd_ref[...].astype(BF16)
    tm, d_model = x_ref.shape
    hd = d_model // MEM_HEADS
    sub = tm // CROSS_SPLIT
    tiles = [dict(r=pl.ds(s * sub, sub), o=[]) for s in range(CROSS_SPLIT)]

    def mix_out(t):
        y = _dot(y_refs[0][t["r"], :], w_refs[0][...])
        for p in range(1, n_parts):
            y = y + _dot(y_refs[p][t["r"], :], w_refs[p][...])
        t["y"] = y

    def query(t):
        t["x1"] = x_ref[t["r"], :] + _rms(t.pop("y"), gpost_ref[...])
        h = _rms(t["x1"], gcpre_ref[...]).astype(BF16)
        t["q"] = (_dot(h, wq_ref[...]) * (hd ** -0.5)).astype(BF16)

    def head(n, t):
        s = _dot(t["q"][:, n * hd:(n + 1) * hd], kt_ref[n * hd:(n + 1) * hd, :])
        e = jnp.exp(s - jnp.max(s, axis=-1, keepdims=True))
        o = _dot(e.astype(BF16), v_ref[:, n * hd:(n + 1) * hd]) / jnp.sum(e, axis=-1, keepdims=True)
        t["o"].append(o.astype(BF16))

    def attn_out(t):
        t["y2"] = _dot(jnp.concatenate(t.pop("o"), axis=-1), wo_ref[...])

    def finish(t):
        xo_ref[t["r"], :] = t.pop("x1") + _rms(t.pop("y2"), gcpost_ref[...])

    phases = ([mix_out, query] + [functools.partial(head, n) for n in range(MEM_HEADS)]
              + [attn_out, finish])
    for phase in phases:
        for t in tiles:
            phase(t)


def _ffn_kernel(x_ref, gpre_ref, wg_ref, wu_ref, wd_ref, g_ref, xo_ref):
    h = _rms(x_ref[...], gpre_ref[...]).astype(BF16)
    acc, pending = None, None

    def down(acc, pending):
        a, c0, cw = pending
        part = _dot(a, wd_ref[c0:c0 + cw, :])
        return part if acc is None else acc + part

    for c0, cw in FFN_CHUNKS:
        g = _dot(h, wg_ref[:, c0:c0 + cw])
        u = _dot(h, wu_ref[:, c0:c0 + cw])
        if pending is not None:
            acc = down(acc, pending)
        pending = ((jax.nn.silu(g) * u).astype(BF16), c0, cw)
    acc = down(acc, pending)
    xo_ref[...] = x_ref[...] + _rms(acc, g_ref[...])


def _params(sem):
    return pltpu.CompilerParams(dimension_semantics=sem, vmem_limit_bytes=VMEM_LIMIT)


def _const_spec(shape, index):
    return pl.BlockSpec(shape, index, pipeline_mode=pl.Buffered(1))


def _layer_vec_spec(row, width):
    return _const_spec((None, 1, width), lambda *_: (row, 0, 0))


def _even_in(x2, vecs, r_pre, w_all, w_t, bf_pad, conv_w, e, tri, pqt, pk, seq, offs):
    n, d_model = x2.shape
    tm = ROW_TILE
    sc_w = offs[2] - offs[1]
    q_rows = pk.shape[1]
    n_heads = q_rows // HEAD_SLOT
    vt_rows = n_heads * VT_SLOT
    tiles_per_seq = seq // tm
    row = lambda i: (i, 0)
    chunked = lambda i: (i // tiles_per_seq, i % tiles_per_seq, 0, 0)
    kern = functools.partial(_even_in_kernel, tiles_per_seq=tiles_per_seq, offs=offs)
    return pl.pallas_call(
        kern,
        grid=(n // tm,),
        in_specs=[
            pl.BlockSpec((tm, d_model), row),
            _layer_vec_spec(r_pre, d_model),
            _const_spec((None, d_model, offs[-1]), lambda i: (e, 0, 0)),
            _const_spec((None, w_t.shape[1], d_model), lambda i: (e, 0, 0)),
            _const_spec((None, 1, LANES), lambda i: (e, 0, 0)),
            _const_spec((None, SC_K, sc_w), lambda i: (e, 0, 0)),
            _const_spec((tm, tm), lambda i: (0, 0)),
            _const_spec(pqt.shape, lambda i: (0, 0)),
            _const_spec((LANES, q_rows), lambda i: (0, 0)),
        ],
        out_specs=[
            pl.BlockSpec((None, tm // ATT_TQ, q_rows, ATT_TQ), chunked),
            pl.BlockSpec((tm, q_rows), row),
            pl.BlockSpec((None, tm // ATT_TK, vt_rows, ATT_TK), chunked),
            pl.BlockSpec((tm, sc_w), row),
        ],
        out_shape=[
            jax.ShapeDtypeStruct((n // seq, seq // ATT_TQ, q_rows, ATT_TQ), BF16),
            jax.ShapeDtypeStruct((n, q_rows), BF16),
            jax.ShapeDtypeStruct((n // seq, seq // ATT_TK, vt_rows, ATT_TK), BF16),
            jax.ShapeDtypeStruct((n, sc_w), BF16),
        ],
        scratch_shapes=[pltpu.VMEM((1, LANES), F32), pltpu.VMEM((SUBLANES, sc_w), F32)],
        compiler_params=_params(("arbitrary",)),
        name="even_in",
    )(x2, vecs, w_all, w_t, bf_pad, conv_w, tri, pqt, pk)


def _fox_attn(qt, ka, vt, seq):
    bsz, nq, _, _ = qt.shape
    _, nkb, vt_rows, _ = vt.shape
    hps = ATT_HEADS_PER_STEP
    n_groups = vt_rows // (hps * VT_SLOT)
    return pl.pallas_call(
        _fox_attn_kernel,
        grid=(bsz, n_groups, nq),
        in_specs=[
            pl.BlockSpec((None, None, hps * HEAD_SLOT, ATT_TQ), lambda b, p, i: (b, i, p, 0)),
            pl.BlockSpec((seq, hps * HEAD_SLOT), lambda b, p, i: (b, p)),
            pl.BlockSpec((None, nkb, hps * VT_SLOT, ATT_TK), lambda b, p, i: (b, 0, p, 0)),
        ],
        out_specs=pl.BlockSpec((ATT_TQ, hps * FOX_HEAD_DIM), lambda b, p, i: (b * nq + i, p)),
        out_shape=jax.ShapeDtypeStruct((bsz * seq, n_groups * hps * FOX_HEAD_DIM), BF16),
        scratch_shapes=[pltpu.VMEM((hps, VT_SLOT, ATT_TQ), F32)],
        compiler_params=_params(("arbitrary", "arbitrary", "arbitrary")),
        name="fox_attn",
    )(qt, ka, vt)


def _lru(x2, vecs, rows, w_in, conv_w, w_a, w_i, o, seq):
    r_pre, r_cb, r_ba, r_bi, r_lam = rows
    n, d_model = x2.shape
    bsz = n // seq
    assert bsz == SUBLANES
    steps = LRU_STEPS
    nb = d_model // LRU_BW
    blk = pl.BlockSpec((bsz, steps, d_model), lambda i: (0, i, 0))
    relayout = pltpu.VMEM((d_model // LANES, bsz * LRU_PITCH, LANES), F32)
    y = pl.pallas_call(
        _lru_kernel,
        grid=(seq // steps,),
        in_specs=[
            blk,
            _layer_vec_spec(r_pre, d_model),
            _const_spec((None, d_model, 2 * d_model), lambda i: (o, 0, 0)),
            _const_spec((None, RG_CONV_K, d_model), lambda i: (o, 0, 0)),
            _layer_vec_spec(r_cb, d_model),
            _const_spec((None, nb, LRU_BW, LRU_BW), lambda i: (o, 0, 0, 0)),
            _layer_vec_spec(r_ba, d_model),
            _const_spec((None, nb, LRU_BW, LRU_BW), lambda i: (o, 0, 0, 0)),
            _layer_vec_spec(r_bi, d_model),
            _layer_vec_spec(r_lam, d_model),
        ],
        out_specs=blk,
        out_shape=jax.ShapeDtypeStruct((bsz, seq, d_model), BF16),
        scratch_shapes=[pltpu.VMEM((bsz, d_model), F32),
                        pltpu.VMEM(((RG_CONV_K - 1) * bsz, d_model), F32),
                        relayout, relayout],
        compiler_params=_params(("arbitrary",)),
        name="rglru",
    )(x2.reshape(bsz, seq, d_model), vecs, w_in, conv_w, vecs, w_a, vecs, w_i, vecs, vecs)
    return y.reshape(n, d_model)


def _mem_kv(mem, vecs, row0, w_xkv):
    bsz, mlen, d_model = mem.shape
    depth = w_xkv.shape[0]
    nb = bsz // MEM_KV_SPLIT
    return pl.pallas_call(
        _mem_kv_kernel,
        grid=(depth, MEM_KV_SPLIT),
        in_specs=[
            pl.BlockSpec((nb, mlen, d_model), lambda l, b: (b, 0, 0)),
            pl.BlockSpec((None, 1, d_model), lambda l, b: (row0 + l, 0, 0)),
            pl.BlockSpec((None, d_model, 2 * d_model), lambda l, b: (l, 0, 0)),
        ],
        out_specs=[
            pl.BlockSpec((None, nb, d_model, mlen), lambda l, b: (l, b, 0, 0)),
            pl.BlockSpec((None, nb, mlen, d_model), lambda l, b: (l, b, 0, 0)),
        ],
        out_shape=[jax.ShapeDtypeStruct((depth, bsz, d_model, mlen), BF16),
                   jax.ShapeDtypeStruct((depth, bsz, mlen, d_model), BF16)],
        compiler_params=_params(("arbitrary", "arbitrary")),
        name="mem_kv",
    )(mem, vecs, w_xkv)


def _cross(y_parts, w_parts, x2, layer, vecs, rows, w_xq, k_mem, v_mem, w_xo, w_gu, w_down,
           seq):
    n, d_model = x2.shape
    d_ff = w_down.shape[1]
    steps = n // CROSS_TILE
    gu_rows, down_rows = d_model // steps, d_ff // steps
    slab = lambda i: (i, 0)
    r_post, r_cpre, r_cpost = rows
    tm = CROSS_TILE
    tiles_per_seq = seq // tm
    mlen = v_mem.shape[2]
    row = lambda i: (i, 0)
    in_specs = [pl.BlockSpec((tm, yp.shape[1]), row) for yp in y_parts]
    w_arrays = []
    for yp, (w_arr, idx, blk) in zip(y_parts, w_parts):
        in_specs.append(_const_spec((None, yp.shape[1], d_model),
                                    lambda i, idx=idx, blk=blk: (idx, blk, 0)))
        w_arrays.append(w_arr)
    mem_index = lambda i: (layer, i // tiles_per_seq, 0, 0)
    in_specs += [
        pl.BlockSpec((tm, d_model), row),
        _layer_vec_spec(r_post, d_model),
        _layer_vec_spec(r_cpre, d_model),
        _const_spec((None, d_model, d_model), lambda i: (layer, 0, 0)),
        pl.BlockSpec((None, None, d_model, mlen), mem_index),
        pl.BlockSpec((None, None, mlen, d_model), mem_index),
        _const_spec((None, d_model, d_model), lambda i: (layer, 0, 0)),
        _layer_vec_spec(r_cpost, d_model),
        pl.BlockSpec((None, gu_rows, 2 * d_ff), lambda i: (layer, i, 0)),
        pl.BlockSpec((None, down_rows, d_model), lambda i: (layer, i, 0)),
    ]
    kern = functools.partial(_cross_kernel, n_parts=len(y_parts))
    return pl.pallas_call(
        kern,
        grid=(steps,),
        in_specs=in_specs,
        out_specs=[pl.BlockSpec((tm, d_model), row),
                   pl.BlockSpec((gu_rows, 2 * d_ff), slab),
                   pl.BlockSpec((down_rows, d_model), slab)],
        out_shape=[jax.ShapeDtypeStruct((n, d_model), F32),
                   jax.ShapeDtypeStruct((d_model, 2 * d_ff), BF16),
                   jax.ShapeDtypeStruct((d_ff, d_model), BF16)],
        compiler_params=_params(("arbitrary",)),
        name="cross",
    )(*y_parts, *w_arrays, x2, vecs, vecs, w_xq, k_mem, v_mem, w_xo, vecs, w_gu, w_down)


def _ffn(x2, w_gu, w_down, vecs, r_pre, r_post):
    n, d_model = x2.shape
    d_ff = w_down.shape[0]
    tm = FFN_TILE
    row = lambda i: (i, 0)
    return pl.pallas_call(
        _ffn_kernel,
        grid=(n // tm,),
        in_specs=[
            pl.BlockSpec((tm, d_model), row),
            _layer_vec_spec(r_pre, d_model),
            _const_spec((d_model, d_ff), lambda i: (0, 0)),
            _const_spec((d_model, d_ff), lambda i: (0, 1)),
            _const_spec((d_ff, d_model), lambda i: (0, 0)),
            _layer_vec_spec(r_post, d_model),
        ],
        out_specs=pl.BlockSpec((tm, d_model), row),
        out_shape=jax.ShapeDtypeStruct((n, d_model), F32),
        compiler_params=_params(("arbitrary",)),
        name="ffn",
    )(x2, vecs, w_gu, w_gu, w_down, vecs)


def _pad_heads(w, n_heads, slot):
    e, d, _ = w.shape
    w = w.reshape(e, d, n_heads, FOX_HEAD_DIM)
    w = jnp.pad(w, ((0, 0), (0, 0), (0, 0), (0, slot - FOX_HEAD_DIM)))
    return w.reshape(e, d, n_heads * slot)


def _bias_placement(n_heads):
    pad_rows = VT_SLOT - FOX_HEAD_DIM
    pqt = [[0.0] * LANES for _ in range(n_heads * pad_rows)]
    pk = [[0.0] * (n_heads * HEAD_SLOT) for _ in range(LANES)]
    one_lane = 24
    for h in range(n_heads):
        base = h * HEAD_SLOT + FOX_HEAD_DIM
        for part in range(3):
            pqt[h * pad_rows + part][part * 8 + h] = 1.0
            pk[one_lane][base + part] = 1.0
            pqt[h * pad_rows + 3 + part][one_lane] = 1.0
            pk[part * 8 + h][base + 3 + part] = -1.0
    return jnp.array(pqt, BF16), jnp.array(pk, BF16)


def kernel(x, mem, g_mix_pre, g_mix_post, g_cross_pre, g_mem, g_cross_post, g_ffn_pre, g_ffn_post, w_xq, w_xkv, w_xo, w_ffn_gu, w_ffn_down, ab_w_in, ab_b_f, ab_conv_w, ab_w_out, c_w_in, c_conv_w, c_conv_b, c_w_a, c_b_a, c_w_i, c_b_i, c_lam, c_w_out):
    bsz, seq, d_model = x.shape
    depth = g_mix_pre.shape[0]
    n = bsz * seq
    n_heads = ab_b_f.shape[1]
    fox_w = n_heads * FOX_HEAD_DIM
    sc_w = ab_conv_w.shape[2]
    assert n_heads <= 8 and seq % ROW_TILE == 0 and seq % LRU_STEPS == 0 and seq % ATT_TQ == 0

    n_odd = c_lam.shape[0]
    gains = (g_mix_pre, g_mix_post, g_cross_pre, g_mem, g_cross_post, g_ffn_pre, g_ffn_post)
    odd_vecs = (c_conv_b, 0.5 * c_b_a.reshape(n_odd, -1), 0.5 * c_b_i.reshape(n_odd, -1), c_lam)
    vecs = jnp.concatenate(gains + odd_vecs, axis=0)[:, None, :]
    (R_MIX_PRE, R_MIX_POST, R_CROSS_PRE, R_MEM, R_CROSS_POST, R_FFN_PRE,
     R_FFN_POST) = (k * depth for k in range(len(gains)))
    R_CONV_B, R_B_A, R_B_I, R_LAM = (len(gains) * depth + k * n_odd for k in range(4))

    i1, i2, i3 = fox_w, 2 * fox_w, 3 * fox_w
    i4 = i3 + n_heads
    i5, i6 = i4 + sc_w, i4 + 2 * sc_w
    w_in_b = ab_w_in.astype(BF16)
    w_f = w_in_b[:, :, i3:i4]
    w_k = _pad_heads(w_in_b[:, :, i1:i2], n_heads, HEAD_SLOT)
    w_k = w_k.at[:, :, F_LANE0:F_LANE0 + 3 * n_heads].set(jnp.concatenate([w_f, w_f, w_f], -1))
    pieces = [w_k, w_in_b[:, :, i4:i5], w_in_b[:, :, i5:i6], w_in_b[:, :, i6:]]
    w_t = jnp.concatenate([w_in_b[:, :, 0:i1], w_in_b[:, :, i2:i3]], axis=-1)
    w_t = w_t.transpose(0, 2, 1)
    offs = []
    for p in pieces:
        offs.append((offs[-1] if offs else 0) + p.shape[-1])
    offs = tuple(offs)
    w_even = jnp.concatenate(pieces, axis=-1)
    b_f3 = jnp.pad(jnp.concatenate([ab_b_f, ab_b_f, ab_b_f], axis=-1),
                   ((0, 0), (0, LANES - 3 * n_heads))).reshape(-1, 1, LANES)
    tri = (lax.broadcasted_iota(jnp.int32, (ROW_TILE, ROW_TILE), 0)
           >= lax.broadcasted_iota(jnp.int32, (ROW_TILE, ROW_TILE), 1)).astype(BF16)
    pqt, pk = _bias_placement(n_heads)

    w_out_even = ab_w_out.astype(BF16)
    w_c_in = c_w_in.astype(BF16)
    w_c_out = c_w_out.astype(BF16)
    w_a = (0.5 * c_w_a).astype(BF16)
    w_i = (0.5 * c_w_i).astype(BF16)
    w_xq_b = w_xq.astype(BF16)
    w_xo_b = w_xo.astype(BF16)

    k_mem, v_mem = _mem_kv(mem, vecs, R_MEM, w_xkv.astype(BF16))

    x2 = x.reshape(n, d_model)
    for layer in range(depth):
        if layer % 2 == 0:
            e = layer // 2
            qt, ka, vt, yb = _even_in(x2, vecs, R_MIX_PRE + layer, w_even, w_t, b_f3,
                                      ab_conv_w, e, tri, pqt, pk, seq, offs)
            ya = _fox_attn(qt, ka, vt, seq)
            y_parts = [ya, yb]
            assert fox_w == sc_w
            w_parts = [(w_out_even, e, 0), (w_out_even, e, 1)]
        else:
            o = layer // 2
            rows = (R_MIX_PRE + layer, R_CONV_B + o, R_B_A + o, R_B_I + o, R_LAM + o)
            ym = _lru(x2, vecs, rows, w_c_in, c_conv_w, w_a, w_i, o, seq)
            y_parts = [ym]
            w_parts = [(w_c_out, o, 0)]
        rows = (R_MIX_POST + layer, R_CROSS_PRE + layer, R_CROSS_POST + layer)
        x2, w_gu_b, w_down_b = _cross(y_parts, w_parts, x2, layer, vecs, rows, w_xq_b,
                                      k_mem, v_mem, w_xo_b, w_ffn_gu, w_ffn_down, seq)
        x2 = _ffn(x2, w_gu_b, w_down_b, vecs, R_FFN_PRE + layer, R_FFN_POST + layer)
    return x2.reshape(bsz, seq, d_model)
```

```python
import functools
import math

import jax
import jax.numpy as jnp
from jax import lax
from jax.experimental import pallas as pl
from jax.experimental.pallas import tpu as pltpu

F32 = jnp.float32
BF16 = jnp.bfloat16

EPS = 1e-6
NEG_INF = -1e30
FOX_HEAD_DIM = 64
SC_K = 3
LRU_BW = 256
RG_CONV_K = 4
RG_C = 8.0
MEM_HEADS = 4

LANES = 128
SUBLANES = 8
HEAD_SLOT = LANES
VT_SLOT = 80
F_LANE0 = FOX_HEAD_DIM + 6
LOG2_E = math.log2(math.e)
VMEM_LIMIT = 56 * 1024 * 1024

ROW_TILE = 512
CROSS_TILE = 1024
CROSS_SPLIT = 4
MEM_KV_SPLIT = 2
LRU_STEPS = 64
LRU_PITCH = LRU_STEPS + SUBLANES
ATT_TQ = 256
ATT_TK = 256
ATT_HEADS_PER_STEP = 8
FFN_TILE = 512
FFN_CHUNKS = ((0, 1024), (1024, 1024), (2048, 768))

NT_DIMS = (((1,), (1,)), ((), ()))


def _dot(a, b):
    return jnp.dot(a, b, preferred_element_type=F32)


def _dot_nt(a, b):
    return lax.dot_general(a, b, NT_DIMS, preferred_element_type=F32)


def _rms(x, g):
    ms = jnp.mean(x * x, axis=-1, keepdims=True)
    return x * lax.rsqrt(ms + EPS) * g


def _gelu_tanh(x):
    k = math.sqrt(2.0 / math.pi)
    half_x = 0.5 * x
    return half_x * jnp.tanh(x * (k + (k * 0.044715) * (x * x))) + half_x


def _split3(x):
    hi = x.astype(BF16).astype(F32)
    r = x - hi
    mid = r.astype(BF16).astype(F32)
    lo = (r - mid).astype(BF16).astype(F32)
    return hi, mid, lo


def _causal_conv(cur, prev_tail, w_ref, k_width):
    head_cur = cur[0:SUBLANES, :]
    row8 = lax.broadcasted_iota(jnp.int32, head_cur.shape, 0)
    main = cur * w_ref[k_width - 1:k_width, :]
    head = head_cur * w_ref[k_width - 1:k_width, :]
    for d in range(1, k_width):
        wk = w_ref[k_width - 1 - d:k_width - d, :]
        main = main + pltpu.roll(cur, d, 0) * wk
        shifted = jnp.where(row8 < d, pltpu.roll(prev_tail, d, 0), pltpu.roll(head_cur, d, 0))
        head = head + shifted * wk
    return main, head


def _even_in_kernel(x_ref, g_ref, w_ref, wt_ref, bf_ref, cw_ref, tri_ref,
                    pqt_ref, pk_ref, qt_ref, ka_ref, vt_ref, yb_ref, carry_c, carry_cu, *,
                    tiles_per_seq, offs):
    o_b, o_c, o_u, o_end = offs
    tm = x_ref.shape[0]
    fox_w = wt_ref.shape[0] // 2
    n_heads = fox_w // FOX_HEAD_DIM
    first = (pl.program_id(0) % tiles_per_seq) == 0

    @pl.when(first)
    def _():
        carry_c[...] = jnp.zeros_like(carry_c)
        carry_cu[...] = jnp.zeros_like(carry_cu)

    h = _rms(x_ref[...], g_ref[...]).astype(BF16)

    lane = lax.broadcasted_iota(jnp.int32, (tm, LANES), 1)
    kp = _dot(h, w_ref[:, 0:o_b])
    scale = FOX_HEAD_DIM ** -0.5 * LOG2_E
    qt = _dot_nt(wt_ref[0:fox_w, :], h) * scale

    fl = pltpu.roll(kp[:, 0:LANES], LANES - F_LANE0, 1)
    fl = jnp.where(lane < 24, fl, 0.0) + bf_ref[...]
    lf = jax.nn.log_sigmoid(fl)
    hi, mid, lo = _split3(lf)
    parts = jnp.where(lane < 8, hi, jnp.where(lane < 16, mid, lo)).astype(BF16)
    vt = _dot_nt(wt_ref[fox_w:, :], h)
    r = _dot(tri_ref[...], parts)
    b_gate = _dot(h, w_ref[:, o_b:o_c])
    cu = _dot(h, w_ref[:, o_c:o_u]) * _dot(h, w_ref[:, o_u:o_end])

    cum = r + pltpu.roll(r, LANES - 8, 1) + pltpu.roll(r, LANES - 16, 1)
    cum = jnp.where(lane < 8, cum + carry_c[...], 0.0)
    carry_c[...] = cum[tm - 1:tm, :]
    chi, cmid, clo = _split3(cum * LOG2_E)
    sel = jnp.where(lane < 8, chi,
                    jnp.where(lane < 16, pltpu.roll(cmid, 8, 1),
                              jnp.where(lane < 24, pltpu.roll(clo, 16, 1),
                                        jnp.where(lane == 24, 1.0, 0.0)))).astype(BF16)

    k0 = jnp.where(lane < FOX_HEAD_DIM, kp[:, 0:LANES], 0.0)
    kp = jnp.concatenate([k0, kp[:, LANES:]], axis=-1)
    ka_ref[...] = (kp + _dot(sel, pk_ref[...])).astype(BF16)

    qb = _dot_nt(pqt_ref[...], sel)
    pad_rows = VT_SLOT - FOX_HEAD_DIM
    q_zero = jnp.zeros((HEAD_SLOT - FOX_HEAD_DIM - pad_rows, tm), F32)
    ones_blk = (lax.broadcasted_iota(jnp.int32, (pad_rows, tm), 0) == 0).astype(F32)
    q_pieces, v_pieces = [], []
    for n in range(n_heads):
        feat = slice(n * FOX_HEAD_DIM, (n + 1) * FOX_HEAD_DIM)
        q_pieces += [qt[feat, :], qb[n * pad_rows:(n + 1) * pad_rows, :], q_zero]
        v_pieces += [vt[feat, :], ones_blk]
    qt = jnp.concatenate(q_pieces, axis=0).astype(BF16)
    vt = jnp.concatenate(v_pieces, axis=0).astype(BF16)
    for c in range(qt_ref.shape[0]):
        w = qt_ref.shape[-1]
        qt_ref[c] = qt[:, c * w:(c + 1) * w]
    for c in range(vt_ref.shape[0]):
        w = vt_ref.shape[-1]
        vt_ref[c] = vt[:, c * w:(c + 1) * w]

    main, head = _causal_conv(cu, carry_cu[...], cw_ref, SC_K)
    carry_cu[...] = cu[tm - SUBLANES:tm, :]
    yb_ref[...] = (b_gate * main).astype(BF16)
    yb_ref[0:SUBLANES, :] = (b_gate[0:SUBLANES, :] * head).astype(BF16)


def _fox_attn_kernel(qt_ref, ka_ref, vt_ref, o_ref, acc_ref):
    tq = qt_ref.shape[-1]
    qi = pl.program_id(2)
    krow = lax.broadcasted_iota(jnp.int32, (ATT_TK, tq), 0)
    qcol = lax.broadcasted_iota(jnp.int32, (ATT_TK, tq), 1)
    heads = range(acc_ref.shape[0])

    def scores(j):
        k0 = pl.multiple_of(j * ATT_TK, ATT_TK)
        return [_dot(ka_ref[pl.ds(k0, ATT_TK), hh * HEAD_SLOT:(hh + 1) * HEAD_SLOT],
                     qt_ref[hh * HEAD_SLOT:(hh + 1) * HEAD_SLOT, :]) for hh in heads]

    def update(j, sts, ms, masked):
        ps, alphas, new = [], [], []
        for hh in heads:
            st = sts[hh]
            if masked:
                st = jnp.where(qcol >= krow, st, NEG_INF)
            m_new = jnp.maximum(ms[hh], jnp.max(st, axis=0, keepdims=True))
            alphas.append(jnp.exp2(ms[hh] - m_new))
            ps.append(jnp.exp2(st - m_new).astype(BF16))
            new.append(m_new)
        for hh in heads:
            vt = vt_ref[j, hh * VT_SLOT:(hh + 1) * VT_SLOT, :]
            acc_ref[hh] = alphas[hh] * acc_ref[hh] + _dot(vt, ps[hh])
        return tuple(new)

    def pair(ja, jb, ms, masked_b):
        sts_a, sts_b = scores(ja), scores(jb)
        return update(jb, sts_b, update(ja, sts_a, ms, False), masked_b)

    acc_ref[...] = jnp.zeros_like(acc_ref)
    ms = tuple(jnp.full((1, tq), NEG_INF, F32) for _ in heads)
    ms = lax.fori_loop(0, qi // 2, lambda jj, c: pair(2 * jj, 2 * jj + 1, c, False), ms)

    @pl.when(qi % 2 == 1)
    def _():
        pair(qi - 1, qi, ms, True)

    @pl.when(qi % 2 == 0)
    def _():
        update(qi, scores(qi), ms, True)

    o_t = jnp.concatenate(
        [acc_ref[hh, 0:FOX_HEAD_DIM, :] / acc_ref[hh, FOX_HEAD_DIM:FOX_HEAD_DIM + 1, :]
         for hh in heads], axis=0)
    o_ref[...] = o_t.T.astype(BF16)


def _lru_kernel(x_ref, g_ref, win_ref, cw_ref, cb_ref, wa_ref, ba_ref, wi_ref, bi_ref,
                lam_ref, y_ref, carry_h, carry_u, in_scr, out_scr):
    bsz, steps, width = x_ref.shape
    tm = steps * bsz
    tail = (RG_CONV_K - 1) * bsz
    lane_tiles = range(width // LANES)

    @pl.when(pl.program_id(0) == 0)
    def _():
        carry_h[...] = jnp.zeros_like(carry_h)
        carry_u[...] = jnp.zeros_like(carry_u)

    for b in range(bsz):
        hb = _rms(x_ref[b], g_ref[...])
        for k in lane_tiles:
            in_scr[k, b * LRU_PITCH:b * LRU_PITCH + steps, :] = hb[:, k * LANES:(k + 1) * LANES]
    h = jnp.concatenate(
        [jnp.concatenate([in_scr[k, pl.ds(t, bsz, stride=LRU_PITCH), :] for k in lane_tiles],
                         axis=-1) for t in range(steps)], axis=0).astype(BF16)
    u = _dot(h, win_ref[:, width:2 * width])

    ext = jnp.concatenate([carry_u[...], u], axis=0)
    carry_u[...] = u[tm - tail:tm, :]
    uc = cb_ref[...] + u * cw_ref[RG_CONV_K - 1:RG_CONV_K, :]
    for d in range(1, RG_CONV_K):
        shifted = ext[tail - d * bsz:tail - d * bsz + tm, :]
        uc = uc + shifted * cw_ref[RG_CONV_K - 1 - d:RG_CONV_K - d, :]
    ucb = uc.astype(BF16)

    nb = width // LRU_BW
    tr = jnp.tanh(jnp.concatenate(
        [_dot(ucb[:, n * LRU_BW:(n + 1) * LRU_BW], wa_ref[n]) for n in range(nb)], axis=-1)
        + ba_ref[...])
    ti = jnp.tanh(jnp.concatenate(
        [_dot(ucb[:, n * LRU_BW:(n + 1) * LRU_BW], wi_ref[n]) for n in range(nb)], axis=-1)
        + bi_ref[...])

    half_c = (0.5 * RG_C) * jax.nn.softplus(-lam_ref[...])
    neg_log_a = tr * half_c + half_c
    a = jnp.exp2(neg_log_a * (-LOG2_E))
    v = jnp.tanh(neg_log_a) * (1.0 + a * a)
    root = jnp.where(v > 0.0, v * lax.rsqrt(v), 0.0)
    b = root * ((ti * 0.5 + 0.5) * uc)

    state = carry_h[...]
    hs = []
    for t in range(steps):
        state = a[t * bsz:(t + 1) * bsz, :] * state + b[t * bsz:(t + 1) * bsz, :]
        hs.append(state)
    carry_h[...] = state
    hs = jnp.concatenate(hs, axis=0)

    y = _gelu_tanh(_dot(h, win_ref[:, 0:width])) * hs
    for t in range(steps):
        for k in lane_tiles:
            out_scr[k, pl.ds(t, bsz, stride=LRU_PITCH), :] = (
                y[t * bsz:(t + 1) * bsz, k * LANES:(k + 1) * LANES])
    for b in range(bsz):
        y_ref[b] = jnp.concatenate(
            [out_scr[k, b * LRU_PITCH:b * LRU_PITCH + steps, :] for k in lane_tiles],
            axis=-1).astype(BF16)


def _mem_kv_kernel(mem_ref, g_ref, w_ref, kt_ref, v_ref):
    nb, mlen, d_model = mem_ref.shape
    m = _rms(mem_ref[...].reshape(nb * mlen, d_model), g_ref[...]).astype(BF16)
    k = _dot(m, w_ref[:, 0:d_model].astype(BF16))
    v = _dot(m, w_ref[:, d_model:2 * d_model].astype(BF16)).astype(BF16)
    for b in range(nb):
        kt_ref[b] = k[b * mlen:(b + 1) * mlen, :].T.astype(BF16)
        v_ref[b] = v[b * mlen:(b + 1) * mlen, :]


def _cross_kernel(*refs, n_parts):
    y_refs = refs[0:n_parts]
    w_refs = refs[n_parts:2 * n_parts]
    (x_ref, gpost_ref, gcpre_ref, wq_ref, kt_ref, v_ref, wo_ref, gcpost_ref,
     wgu_ref, wd_ref, xo_ref, wgu_out_ref, wd_out_ref) = refs[2 * n_parts:]
    wgu_out_ref[...] = wgu_ref[...].astype(BF16)
    wd_out_ref[...] = wd_ref[...].astype(BF16)
    tm, d_model = x_ref.shape
    hd = d_model // MEM_HEADS
    sub = tm // CROSS_SPLIT
    tiles = [dict(r=pl.ds(s * sub, sub), o=[]) for s in range(CROSS_SPLIT)]

    def mix_out(t):
        y = _dot(y_refs[0][t["r"], :], w_refs[0][...])
        for p in range(1, n_parts):
            y = y + _dot(y_refs[p][t["r"], :], w_refs[p][...])
        t["y"] = y

    def query(t):
        t["x1"] = x_ref[t["r"], :] + _rms(t.pop("y"), gpost_ref[...])
        h = _rms(t["x1"], gcpre_ref[...]).astype(BF16)
        t["q"] = (_dot(h, wq_ref[...]) * (hd ** -0.5)).astype(BF16)

    def head(n, t):
        s = _dot(t["q"][:, n * hd:(n + 1) * hd], kt_ref[n * hd:(n + 1) * hd, :])
        e = jnp.exp(s - jnp.max(s, axis=-1, keepdims=True))
        o = _dot(e.astype(BF16), v_ref[:, n * hd:(n + 1) * hd]) / jnp.sum(e, axis=-1, keepdims=True)
        t["o"].append(o.astype(BF16))

    def attn_out(t):
        t["y2"] = _dot(jnp.concatenate(t.pop("o"), axis=-1), wo_ref[...])

    def finish(t):
        xo_ref[t["r"], :] = t.pop("x1") + _rms(t.pop("y2"), gcpost_ref[...])

    phases = ([mix_out, query] + [functools.partial(head, n) for n in range(MEM_HEADS)]
              + [attn_out, finish])
    for phase in phases:
        for t in tiles:
            phase(t)


def _ffn_kernel(x_ref, gpre_ref, wg_ref, wu_ref, wd_ref, g_ref, xo_ref):
    h = _rms(x_ref[...], gpre_ref[...]).astype(BF16)
    acc, pending = None, None

    def down(acc, pending):
        a, c0, cw = pending
        part = _dot(a, wd_ref[c0:c0 + cw, :])
        return part if acc is None else acc + part

    for c0, cw in FFN_CHUNKS:
        g = _dot(h, wg_ref[:, c0:c0 + cw])
        u = _dot(h, wu_ref[:, c0:c0 + cw])
        if pending is not None:
            acc = down(acc, pending)
        pending = ((jax.nn.silu(g) * u).astype(BF16), c0, cw)
    acc = down(acc, pending)
    xo_ref[...] = x_ref[...] + _rms(acc, g_ref[...])


def _params(sem):
    return pltpu.CompilerParams(dimension_semantics=sem, vmem_limit_bytes=VMEM_LIMIT)


def _const_spec(shape, index):
    return pl.BlockSpec(shape, index, pipeline_mode=pl.Buffered(1))


def _layer_vec_spec(row, width):
    return _const_spec((None, 1, width), lambda *_: (row, 0, 0))


def _even_in(x2, vecs, r_pre, w_all, w_t, bf_pad, conv_w, e, tri, pqt, pk, seq, offs):
    n, d_model = x2.shape
    tm = ROW_TILE
    sc_w = offs[2] - offs[1]
    q_rows = pk.shape[1]
    n_heads = q_rows // HEAD_SLOT
    vt_rows = n_heads * VT_SLOT
    tiles_per_seq = seq // tm
    row = lambda i: (i, 0)
    chunked = lambda i: (i // tiles_per_seq, i % tiles_per_seq, 0, 0)
    kern = functools.partial(_even_in_kernel, tiles_per_seq=tiles_per_seq, offs=offs)
    return pl.pallas_call(
        kern,
        grid=(n // tm,),
        in_specs=[
            pl.BlockSpec((tm, d_model), row),
            _layer_vec_spec(r_pre, d_model),
            _const_spec((None, d_model, offs[-1]), lambda i: (e, 0, 0)),
            _const_spec((None, w_t.shape[1], d_model), lambda i: (e, 0, 0)),
            _const_spec((None, 1, LANES), lambda i: (e, 0, 0)),
            _const_spec((None, SC_K, sc_w), lambda i: (e, 0, 0)),
            _const_spec((tm, tm), lambda i: (0, 0)),
            _const_spec(pqt.shape, lambda i: (0, 0)),
            _const_spec((LANES, q_rows), lambda i: (0, 0)),
        ],
        out_specs=[
            pl.BlockSpec((None, tm // ATT_TQ, q_rows, ATT_TQ), chunked),
            pl.BlockSpec((tm, q_rows), row),
            pl.BlockSpec((None, tm // ATT_TK, vt_rows, ATT_TK), chunked),
            pl.BlockSpec((tm, sc_w), row),
        ],
        out_shape=[
            jax.ShapeDtypeStruct((n // seq, seq // ATT_TQ, q_rows, ATT_TQ), BF16),
            jax.ShapeDtypeStruct((n, q_rows), BF16),
            jax.ShapeDtypeStruct((n // seq, seq // ATT_TK, vt_rows, ATT_TK), BF16),
            jax.ShapeDtypeStruct((n, sc_w), BF16),
        ],
        scratch_shapes=[pltpu.VMEM((1, LANES), F32), pltpu.VMEM((SUBLANES, sc_w), F32)],
        compiler_params=_params(("arbitrary",)),
        name="even_in",
    )(x2, vecs, w_all, w_t, bf_pad, conv_w, tri, pqt, pk)


def _fox_attn(qt, ka, vt, seq):
    bsz, nq, _, _ = qt.shape
    _, nkb, vt_rows, _ = vt.shape
    hps = ATT_HEADS_PER_STEP
    n_groups = vt_rows // (hps * VT_SLOT)
    return pl.pallas_call(
        _fox_attn_kernel,
        grid=(bsz, n_groups, nq),
        in_specs=[
            pl.BlockSpec((None, None, hps * HEAD_SLOT, ATT_TQ), lambda b, p, i: (b, i, p, 0)),
            pl.BlockSpec((seq, hps * HEAD_SLOT), lambda b, p, i: (b, p)),
            pl.BlockSpec((None, nkb, hps * VT_SLOT, ATT_TK), lambda b, p, i: (b, 0, p, 0)),
        ],
        out_specs=pl.BlockSpec((ATT_TQ, hps * FOX_HEAD_DIM), lambda b, p, i: (b * nq + i, p)),
        out_shape=jax.ShapeDtypeStruct((bsz * seq, n_groups * hps * FOX_HEAD_DIM), BF16),
        scratch_shapes=[pltpu.VMEM((hps, VT_SLOT, ATT_TQ), F32)],
        compiler_params=_params(("arbitrary", "arbitrary", "arbitrary")),
        name="fox_attn",
    )(qt, ka, vt)


def _lru(x2, vecs, rows, w_in, conv_w, w_a, w_i, o, seq):
    r_pre, r_cb, r_ba, r_bi, r_lam = rows
    n, d_model = x2.shape
    bsz = n // seq
    assert bsz == SUBLANES
    steps = LRU_STEPS
    nb = d_model // LRU_BW
    blk = pl.BlockSpec((bsz, steps, d_model), lambda i: (0, i, 0))
    relayout = pltpu.VMEM((d_model // LANES, bsz * LRU_PITCH, LANES), F32)
    y = pl.pallas_call(
        _lru_kernel,
        grid=(seq // steps,),
        in_specs=[
            blk,
            _layer_vec_spec(r_pre, d_model),
            _const_spec((None, d_model, 2 * d_model), lambda i: (o, 0, 0)),
            _const_spec((None, RG_CONV_K, d_model), lambda i: (o, 0, 0)),
            _layer_vec_spec(r_cb, d_model),
            _const_spec((None, nb, LRU_BW, LRU_BW), lambda i: (o, 0, 0, 0)),
            _layer_vec_spec(r_ba, d_model),
            _const_spec((None, nb, LRU_BW, LRU_BW), lambda i: (o, 0, 0, 0)),
            _layer_vec_spec(r_bi, d_model),
            _layer_vec_spec(r_lam, d_model),
        ],
        out_specs=blk,
        out_shape=jax.ShapeDtypeStruct((bsz, seq, d_model), BF16),
        scratch_shapes=[pltpu.VMEM((bsz, d_model), F32),
                        pltpu.VMEM(((RG_CONV_K - 1) * bsz, d_model), F32),
                        relayout, relayout],
        compiler_params=_params(("arbitrary",)),
        name="rglru",
    )(x2.reshape(bsz, seq, d_model), vecs, w_in, conv_w, vecs, w_a, vecs, w_i, vecs, vecs)
    return y.reshape(n, d_model)


def _mem_kv(mem, vecs, row0, w_xkv):
    bsz, mlen, d_model = mem.shape
    depth = w_xkv.shape[0]
    nb = bsz // MEM_KV_SPLIT
    return pl.pallas_call(
        _mem_kv_kernel,
        grid=(depth, MEM_KV_SPLIT),
        in_specs=[
            pl.BlockSpec((nb, mlen, d_model), lambda l, b: (b, 0, 0)),
            pl.BlockSpec((None, 1, d_model), lambda l, b: (row0 + l, 0, 0)),
            pl.BlockSpec((None, d_model, 2 * d_model), lambda l, b: (l, 0, 0)),
        ],
        out_specs=[
            pl.BlockSpec((None, nb, d_model, mlen), lambda l, b: (l, b, 0, 0)),
            pl.BlockSpec((None, nb, mlen, d_model), lambda l, b: (l, b, 0, 0)),
        ],
        out_shape=[jax.ShapeDtypeStruct((depth, bsz, d_model, mlen), BF16),
                   jax.ShapeDtypeStruct((depth, bsz, mlen, d_model), BF16)],
        compiler_params=_params(("arbitrary", "arbitrary")),
        name="mem_kv",
    )(mem, vecs, w_xkv)


def _cross(y_parts, w_parts, x2, layer, vecs, rows, w_xq, k_mem, v_mem, w_xo, w_gu, w_down,
           seq):
    n, d_model = x2.shape
    d_ff = w_down.shape[1]
    steps = n // CROSS_TILE
    gu_rows, down_rows = d_model // steps, d_ff // steps
    slab = lambda i: (i, 0)
    r_post, r_cpre, r_cpost = rows
    tm = CROSS_TILE
    tiles_per_seq = seq // tm
    mlen = v_mem.shape[2]
    row = lambda i: (i, 0)
    in_specs = [pl.BlockSpec((tm, yp.shape[1]), row) for yp in y_parts]
    w_arrays = []
    for yp, (w_arr, idx, blk) in zip(y_parts, w_parts):
        in_specs.append(_const_spec((None, yp.shape[1], d_model),
                                    lambda i, idx=idx, blk=blk: (idx, blk, 0)))
        w_arrays.append(w_arr)
    mem_index = lambda i: (layer, i // tiles_per_seq, 0, 0)
    in_specs += [
        pl.BlockSpec((tm, d_model), row),
        _layer_vec_spec(r_post, d_model),
        _layer_vec_spec(r_cpre, d_model),
        _const_spec((None, d_model, d_model), lambda i: (layer, 0, 0)),
        pl.BlockSpec((None, None, d_model, mlen), mem_index),
        pl.BlockSpec((None, None, mlen, d_model), mem_index),
        _const_spec((None, d_model, d_model), lambda i: (layer, 0, 0)),
        _layer_vec_spec(r_cpost, d_model),
        pl.BlockSpec((None, gu_rows, 2 * d_ff), lambda i: (layer, i, 0)),
        pl.BlockSpec((None, down_rows, d_model), lambda i: (layer, i, 0)),
    ]
    kern = functools.partial(_cross_kernel, n_parts=len(y_parts))
    return pl.pallas_call(
        kern,
        grid=(steps,),
        in_specs=in_specs,
        out_specs=[pl.BlockSpec((tm, d_model), row),
                   pl.BlockSpec((gu_rows, 2 * d_ff), slab),
                   pl.BlockSpec((down_rows, d_model), slab)],
        out_shape=[jax.ShapeDtypeStruct((n, d_model), F32),
                   jax.ShapeDtypeStruct((d_model, 2 * d_ff), BF16),
                   jax.ShapeDtypeStruct((d_ff, d_model), BF16)],
        compiler_params=_params(("arbitrary",)),
        name="cross",
    )(*y_parts, *w_arrays, x2, vecs, vecs, w_xq, k_mem, v_mem, w_xo, vecs, w_gu, w_down)


def _ffn(x2, w_gu, w_down, vecs, r_pre, r_post):
    n, d_model = x2.shape
    d_ff = w_down.shape[0]
    tm = FFN_TILE
    row = lambda i: (i, 0)
    return pl.pallas_call(
        _ffn_kernel,
        grid=(n // tm,),
        in_specs=[
            pl.BlockSpec((tm, d_model), row),
            _layer_vec_spec(r_pre, d_model),
            _const_spec((d_model, d_ff), lambda i: (0, 0)),
            _const_spec((d_model, d_ff), lambda i: (0, 1)),
            _const_spec((d_ff, d_model), lambda i: (0, 0)),
            _layer_vec_spec(r_post, d_model),
        ],
        out_specs=pl.BlockSpec((tm, d_model), row),
        out_shape=jax.ShapeDtypeStruct((n, d_model), F32),
        compiler_params=_params(("arbitrary",)),
        name="ffn",
    )(x2, vecs, w_gu, w_gu, w_down, vecs)


def _pad_heads(w, n_heads, slot):
    e, d, _ = w.shape
    w = w.reshape(e, d, n_heads, FOX_HEAD_DIM)
    w = jnp.pad(w, ((0, 0), (0, 0), (0, 0), (0, slot - FOX_HEAD_DIM)))
    return w.reshape(e, d, n_heads * slot)


def _bias_placement(n_heads):
    pad_rows = VT_SLOT - FOX_HEAD_DIM
    pqt = [[0.0] * LANES for _ in range(n_heads * pad_rows)]
    pk = [[0.0] * (n_heads * HEAD_SLOT) for _ in range(LANES)]
    one_lane = 24
    for h in range(n_heads):
        base = h * HEAD_SLOT + FOX_HEAD_DIM
        for part in range(3):
            pqt[h * pad_rows + part][part * 8 + h] = 1.0
            pk[one_lane][base + part] = 1.0
            pqt[h * pad_rows + 3 + part][one_lane] = 1.0
            pk[part * 8 + h][base + 3 + part] = -1.0
    return jnp.array(pqt, BF16), jnp.array(pk, BF16)


def kernel(x, mem, g_mix_pre, g_mix_post, g_cross_pre, g_mem, g_cross_post, g_ffn_pre, g_ffn_post, w_xq, w_xkv, w_xo, w_ffn_gu, w_ffn_down, ab_w_in, ab_b_f, ab_conv_w, ab_w_out, c_w_in, c_conv_w, c_conv_b, c_w_a, c_b_a, c_w_i, c_b_i, c_lam, c_w_out):
    bsz, seq, d_model = x.shape
    depth = g_mix_pre.shape[0]
    n = bsz * seq
    n_heads = ab_b_f.shape[1]
    fox_w = n_heads * FOX_HEAD_DIM
    sc_w = ab_conv_w.shape[2]
    assert n_heads <= 8 and seq % ROW_TILE == 0 and seq % LRU_STEPS == 0 and seq % ATT_TQ == 0

    n_odd = c_lam.shape[0]
    gains = (g_mix_pre, g_mix_post, g_cross_pre, g_mem, g_cross_post, g_ffn_pre, g_ffn_post)
    odd_vecs = (c_conv_b, 0.5 * c_b_a.reshape(n_odd, -1), 0.5 * c_b_i.reshape(n_odd, -1), c_lam)
    vecs = jnp.concatenate(gains + odd_vecs, axis=0)[:, None, :]
    (R_MIX_PRE, R_MIX_POST, R_CROSS_PRE, R_MEM, R_CROSS_POST, R_FFN_PRE,
     R_FFN_POST) = (k * depth for k in range(len(gains)))
    R_CONV_B, R_B_A, R_B_I, R_LAM = (len(gains) * depth + k * n_odd for k in range(4))

    i1, i2, i3 = fox_w, 2 * fox_w, 3 * fox_w
    i4 = i3 + n_heads
    i5, i6 = i4 + sc_w, i4 + 2 * sc_w
    w_in_b = ab_w_in.astype(BF16)
    w_f = w_in_b[:, :, i3:i4]
    w_k = _pad_heads(w_in_b[:, :, i1:i2], n_heads, HEAD_SLOT)
    w_k = w_k.at[:, :, F_LANE0:F_LANE0 + 3 * n_heads].set(jnp.concatenate([w_f, w_f, w_f], -1))
    pieces = [w_k, w_in_b[:, :, i4:i5], w_in_b[:, :, i5:i6], w_in_b[:, :, i6:]]
    w_t = jnp.concatenate([w_in_b[:, :, 0:i1], w_in_b[:, :, i2:i3]], axis=-1)
    w_t = w_t.transpose(0, 2, 1)
    offs = []
    for p in pieces:
        offs.append((offs[-1] if offs else 0) + p.shape[-1])
    offs = tuple(offs)
    w_even = jnp.concatenate(pieces, axis=-1)
    b_f3 = jnp.pad(jnp.concatenate([ab_b_f, ab_b_f, ab_b_f], axis=-1),
                   ((0, 0), (0, LANES - 3 * n_heads))).reshape(-1, 1, LANES)
    tri = (lax.broadcasted_iota(jnp.int32, (ROW_TILE, ROW_TILE), 0)
           >= lax.broadcasted_iota(jnp.int32, (ROW_TILE, ROW_TILE), 1)).astype(BF16)
    pqt, pk = _bias_placement(n_heads)

    w_out_even = ab_w_out.astype(BF16)
    w_c_in = c_w_in.astype(BF16)
    w_c_out = c_w_out.astype(BF16)
    w_a = (0.5 * c_w_a).astype(BF16)
    w_i = (0.5 * c_w_i).astype(BF16)
    w_xq_b = w_xq.astype(BF16)
    w_xo_b = w_xo.astype(BF16)

    k_mem, v_mem = _mem_kv(mem, vecs, R_MEM, w_xkv)

    x2 = x.reshape(n, d_model)
    for layer in range(depth):
        if layer % 2 == 0:
            e = layer // 2
            qt, ka, vt, yb = _even_in(x2, vecs, R_MIX_PRE + layer, w_even, w_t, b_f3,
                                      ab_conv_w, e, tri, pqt, pk, seq, offs)
            ya = _fox_attn(qt, ka, vt, seq)
            y_parts = [ya, yb]
            assert fox_w == sc_w
            w_parts = [(w_out_even, e, 0), (w_out_even, e, 1)]
        else:
            o = layer // 2
            rows = (R_MIX_PRE + layer, R_CONV_B + o, R_B_A + o, R_B_I + o, R_LAM + o)
            ym = _lru(x2, vecs, rows, w_c_in, c_conv_w, w_a, w_i, o, seq)
            y_parts = [ym]
            w_parts = [(w_c_out, o, 0)]
        rows = (R_MIX_POST + layer, R_CROSS_PRE + layer, R_CROSS_POST + layer)
        x2, w_gu_b, w_down_b = _cross(y_parts, w_parts, x2, layer, vecs, rows, w_xq_b,
                                      k_mem, v_mem, w_xo_b, w_ffn_gu, w_ffn_down, seq)
        x2 = _ffn(x2, w_gu_b, w_down_b, vecs, R_FFN_PRE + layer, R_FFN_POST + layer)
    return x2.reshape(bsz, seq, d_model)
```

```python
import functools
import math

import jax
import jax.numpy as jnp
from jax import lax
from jax.experimental import pallas as pl
from jax.experimental.pallas import tpu as pltpu

F32 = jnp.float32
BF16 = jnp.bfloat16

EPS = 1e-6
NEG_INF = -1e30
FOX_HEAD_DIM = 64
SC_K = 3
LRU_BW = 256
RG_CONV_K = 4
RG_C = 8.0
MEM_HEADS = 4

LANES = 128
SUBLANES = 8
HEAD_SLOT = LANES
VT_SLOT = 80
F_LANE0 = FOX_HEAD_DIM + 6
LOG2_E = math.log2(math.e)
VMEM_LIMIT = 56 * 1024 * 1024

ROW_TILE = 512
CROSS_TILE = 1024
CROSS_SPLIT = 4
MEM_KV_SPLIT = 2
LRU_STEPS = 64
LRU_CHUNK = 256
LRU_PITCH = LRU_STEPS + SUBLANES
ATT_TQ = 256
ATT_TK = 256
ATT_HEADS_PER_STEP = 8
FFN_TILE = 512
FFN_CHUNKS = ((0, 1024), (1024, 1024), (2048, 768))

NT_DIMS = (((1,), (1,)), ((), ()))


def _dot(a, b):
    return jnp.dot(a, b, preferred_element_type=F32)


def _dot_nt(a, b):
    return lax.dot_general(a, b, NT_DIMS, preferred_element_type=F32)


def _rms(x, g):
    ms = jnp.mean(x * x, axis=-1, keepdims=True)
    return x * lax.rsqrt(ms + EPS) * g


def _gelu_tanh(x):
    k = math.sqrt(2.0 / math.pi)
    half_x = 0.5 * x
    return half_x * jnp.tanh(x * (k + (k * 0.044715) * (x * x))) + half_x


def _split3(x):
    hi = x.astype(BF16).astype(F32)
    r = x - hi
    mid = r.astype(BF16).astype(F32)
    lo = (r - mid).astype(BF16).astype(F32)
    return hi, mid, lo


def _causal_conv(cur, prev_tail, w_ref, k_width):
    head_cur = cur[0:SUBLANES, :]
    row8 = lax.broadcasted_iota(jnp.int32, head_cur.shape, 0)
    main = cur * w_ref[k_width - 1:k_width, :]
    head = head_cur * w_ref[k_width - 1:k_width, :]
    for d in range(1, k_width):
        wk = w_ref[k_width - 1 - d:k_width - d, :]
        main = main + pltpu.roll(cur, d, 0) * wk
        shifted = jnp.where(row8 < d, pltpu.roll(prev_tail, d, 0), pltpu.roll(head_cur, d, 0))
        head = head + shifted * wk
    return main, head


def _even_in_kernel(x_ref, g_ref, w_ref, wt_ref, bf_ref, cw_ref, tri_ref,
                    pqt_ref, pk_ref, qt_ref, ka_ref, vt_ref, yb_ref, carry_c, carry_cu, *,
                    tiles_per_seq, offs):
    o_b, o_c, o_u, o_end = offs
    tm = x_ref.shape[0]
    fox_w = wt_ref.shape[0] // 2
    n_heads = fox_w // FOX_HEAD_DIM
    first = (pl.program_id(0) % tiles_per_seq) == 0

    @pl.when(first)
    def _():
        carry_c[...] = jnp.zeros_like(carry_c)
        carry_cu[...] = jnp.zeros_like(carry_cu)

    h = _rms(x_ref[...], g_ref[...]).astype(BF16)

    lane = lax.broadcasted_iota(jnp.int32, (tm, LANES), 1)
    kp = _dot(h, w_ref[:, 0:o_b])
    scale = FOX_HEAD_DIM ** -0.5 * LOG2_E
    qt = _dot_nt(wt_ref[0:fox_w, :], h) * scale

    fl = pltpu.roll(kp[:, 0:LANES], LANES - F_LANE0, 1)
    fl = jnp.where(lane < 24, fl, 0.0) + bf_ref[...]
    lf = jax.nn.log_sigmoid(fl)
    hi, mid, lo = _split3(lf)
    parts = jnp.where(lane < 8, hi, jnp.where(lane < 16, mid, lo)).astype(BF16)
    vt = _dot_nt(wt_ref[fox_w:, :], h)
    r = _dot(tri_ref[...], parts)
    b_gate = _dot(h, w_ref[:, o_b:o_c])
    cu = _dot(h, w_ref[:, o_c:o_u]) * _dot(h, w_ref[:, o_u:o_end])

    cum = r + pltpu.roll(r, LANES - 8, 1) + pltpu.roll(r, LANES - 16, 1)
    cum = jnp.where(lane < 8, cum + carry_c[...], 0.0)
    carry_c[...] = cum[tm - 1:tm, :]
    chi, cmid, clo = _split3(cum * LOG2_E)
    sel = jnp.where(lane < 8, chi,
                    jnp.where(lane < 16, pltpu.roll(cmid, 8, 1),
                              jnp.where(lane < 24, pltpu.roll(clo, 16, 1),
                                        jnp.where(lane == 24, 1.0, 0.0)))).astype(BF16)

    k0 = jnp.where(lane < FOX_HEAD_DIM, kp[:, 0:LANES], 0.0)
    kp = jnp.concatenate([k0, kp[:, LANES:]], axis=-1)
    ka_ref[...] = (kp + _dot(sel, pk_ref[...])).astype(BF16)

    qb = _dot_nt(pqt_ref[...], sel)
    pad_rows = VT_SLOT - FOX_HEAD_DIM
    q_zero = jnp.zeros((HEAD_SLOT - FOX_HEAD_DIM - pad_rows, tm), F32)
    ones_blk = (lax.broadcasted_iota(jnp.int32, (pad_rows, tm), 0) == 0).astype(F32)
    q_pieces, v_pieces = [], []
    for n in range(n_heads):
        feat = slice(n * FOX_HEAD_DIM, (n + 1) * FOX_HEAD_DIM)
        q_pieces += [qt[feat, :], qb[n * pad_rows:(n + 1) * pad_rows, :], q_zero]
        v_pieces += [vt[feat, :], ones_blk]
    qt = jnp.concatenate(q_pieces, axis=0).astype(BF16)
    vt = jnp.concatenate(v_pieces, axis=0).astype(BF16)
    for c in range(qt_ref.shape[0]):
        w = qt_ref.shape[-1]
        qt_ref[c] = qt[:, c * w:(c + 1) * w]
    for c in range(vt_ref.shape[0]):
        w = vt_ref.shape[-1]
        vt_ref[c] = vt[:, c * w:(c + 1) * w]

    main, head = _causal_conv(cu, carry_cu[...], cw_ref, SC_K)
    carry_cu[...] = cu[tm - SUBLANES:tm, :]
    yb_ref[...] = (b_gate * main).astype(BF16)
    yb_ref[0:SUBLANES, :] = (b_gate[0:SUBLANES, :] * head).astype(BF16)


def _fox_attn_kernel(qt_ref, ka_ref, vt_ref, o_ref, acc_ref):
    tq = qt_ref.shape[-1]
    qi = pl.program_id(2)
    krow = lax.broadcasted_iota(jnp.int32, (ATT_TK, tq), 0)
    qcol = lax.broadcasted_iota(jnp.int32, (ATT_TK, tq), 1)
    heads = range(acc_ref.shape[0])

    def scores(j):
        k0 = pl.multiple_of(j * ATT_TK, ATT_TK)
        return [_dot(ka_ref[pl.ds(k0, ATT_TK), hh * HEAD_SLOT:(hh + 1) * HEAD_SLOT],
                     qt_ref[hh * HEAD_SLOT:(hh + 1) * HEAD_SLOT, :]) for hh in heads]

    def update(j, sts, ms, masked):
        ps, alphas, new = [], [], []
        for hh in heads:
            st = sts[hh]
            if masked:
                st = jnp.where(qcol >= krow, st, NEG_INF)
            m_new = jnp.maximum(ms[hh], jnp.max(st, axis=0, keepdims=True))
            alphas.append(jnp.exp2(ms[hh] - m_new))
            ps.append(jnp.exp2(st - m_new).astype(BF16))
            new.append(m_new)
        for hh in heads:
            vt = vt_ref[j, hh * VT_SLOT:(hh + 1) * VT_SLOT, :]
            acc_ref[hh] = alphas[hh] * acc_ref[hh] + _dot(vt, ps[hh])
        return tuple(new)

    def pair(ja, jb, ms, masked_b):
        sts_a, sts_b = scores(ja), scores(jb)
        return update(jb, sts_b, update(ja, sts_a, ms, False), masked_b)

    acc_ref[...] = jnp.zeros_like(acc_ref)
    ms = tuple(jnp.full((1, tq), NEG_INF, F32) for _ in heads)
    ms = lax.fori_loop(0, qi // 2, lambda jj, c: pair(2 * jj, 2 * jj + 1, c, False), ms)

    @pl.when(qi % 2 == 1)
    def _():
        pair(qi - 1, qi, ms, True)

    @pl.when(qi % 2 == 0)
    def _():
        update(qi, scores(qi), ms, True)

    o_t = jnp.concatenate(
        [acc_ref[hh, 0:FOX_HEAD_DIM, :] / acc_ref[hh, FOX_HEAD_DIM:FOX_HEAD_DIM + 1, :]
         for hh in heads], axis=0)
    o_ref[...] = o_t.T.astype(BF16)


def _lru_kernel(x_ref, g_ref, win_ref, cw_ref, cb_ref, wa_ref, ba_ref, wi_ref, bi_ref,
                lam_ref, y_ref, carry_h, carry_u, in_scr, out_scr):
    bsz, steps, width = x_ref.shape
    tm = steps * bsz
    tail = (RG_CONV_K - 1) * bsz
    lane_tiles = range(width // LANES)

    @pl.when(pl.program_id(0) == 0)
    def _():
        carry_h[...] = jnp.zeros_like(carry_h)
        carry_u[...] = jnp.zeros_like(carry_u)

    for b in range(bsz):
        hb = _rms(x_ref[b], g_ref[...])
        for k in lane_tiles:
            in_scr[k, b * LRU_PITCH:b * LRU_PITCH + steps, :] = hb[:, k * LANES:(k + 1) * LANES]
    h = jnp.concatenate(
        [jnp.concatenate([in_scr[k, pl.ds(t, bsz, stride=LRU_PITCH), :] for k in lane_tiles],
                         axis=-1) for t in range(steps)], axis=0).astype(BF16)
    half_c_all = (0.5 * RG_C) * jax.nn.softplus(-lam_ref[...])
    for c0 in range(0, width, LRU_CHUNK):
        cs = slice(c0, c0 + LRU_CHUNK)
        u = _dot(h, win_ref[:, width + c0:width + c0 + LRU_CHUNK])

        ext = jnp.concatenate([carry_u[:, cs], u], axis=0)
        carry_u[:, cs] = u[tm - tail:tm, :]
        uc = cb_ref[:, cs] + u * cw_ref[RG_CONV_K - 1:RG_CONV_K, cs]
        for d in range(1, RG_CONV_K):
            shifted = ext[tail - d * bsz:tail - d * bsz + tm, :]
            uc = uc + shifted * cw_ref[RG_CONV_K - 1 - d:RG_CONV_K - d, cs]
        ucb = uc.astype(BF16)

        blocks = range(c0 // LRU_BW, (c0 + LRU_CHUNK) // LRU_BW)
        lo = lambda n: n * LRU_BW - c0
        tr = jnp.tanh(jnp.concatenate(
            [_dot(ucb[:, lo(n):lo(n) + LRU_BW], wa_ref[n]) for n in blocks], axis=-1)
            + ba_ref[:, cs])
        ti = jnp.tanh(jnp.concatenate(
            [_dot(ucb[:, lo(n):lo(n) + LRU_BW], wi_ref[n]) for n in blocks], axis=-1)
            + bi_ref[:, cs])

        half_c = half_c_all[:, cs]
        neg_log_a = tr * half_c + half_c
        a = jnp.exp2(neg_log_a * (-LOG2_E))
        v = jnp.tanh(neg_log_a) * (1.0 + a * a)
        root = jnp.where(v > 0.0, v * lax.rsqrt(v), 0.0)
        b = root * ((ti * 0.5 + 0.5) * uc)

        state = carry_h[:, cs]
        hs = []
        for t in range(steps):
            state = a[t * bsz:(t + 1) * bsz, :] * state + b[t * bsz:(t + 1) * bsz, :]
            hs.append(state)
        carry_h[:, cs] = state
        hs = jnp.concatenate(hs, axis=0)

        y = _gelu_tanh(_dot(h, win_ref[:, c0:c0 + LRU_CHUNK])) * hs
        for t in range(steps):
            for k in range(LRU_CHUNK // LANES):
                out_scr[c0 // LANES + k, pl.ds(t, bsz, stride=LRU_PITCH), :] = (
                    y[t * bsz:(t + 1) * bsz, k * LANES:(k + 1) * LANES])
    for b in range(bsz):
        y_ref[b] = jnp.concatenate(
            [out_scr[k, b * LRU_PITCH:b * LRU_PITCH + steps, :] for k in lane_tiles],
            axis=-1).astype(BF16)


def _mem_kv_kernel(mem_ref, g_ref, w_ref, kt_ref, v_ref):
    nb, mlen, d_model = mem_ref.shape
    m = _rms(mem_ref[...].reshape(nb * mlen, d_model), g_ref[...]).astype(BF16)
    k = _dot(m, w_ref[:, 0:d_model].astype(BF16))
    v = _dot(m, w_ref[:, d_model:2 * d_model].astype(BF16)).astype(BF16)
    for b in range(nb):
        kt_ref[b] = k[b * mlen:(b + 1) * mlen, :].T.astype(BF16)
        v_ref[b] = v[b * mlen:(b + 1) * mlen, :]


def _cross_kernel(*refs, n_parts):
    y_refs = refs[0:n_parts]
    w_refs = refs[n_parts:2 * n_parts]
    (x_ref, gpost_ref, gcpre_ref, wq_ref, kt_ref, v_ref, wo_ref, gcpost_ref,
     wgu_ref, wd_ref, xo_ref, wgu_out_ref, wd_out_ref) = refs[2 * n_parts:]
    wgu_out_ref[...] = wgu_ref[...].astype(BF16)
    wd_out_ref[...] = wd_ref[...].astype(BF16)
    tm, d_model = x_ref.shape
    hd = d_model // MEM_HEADS
    sub = tm // CROSS_SPLIT
    tiles = [dict(r=pl.ds(s * sub, sub), o=[]) for s in range(CROSS_SPLIT)]

    def mix_out(t):
        y = _dot(y_refs[0][t["r"], :], w_refs[0][...])
        for p in range(1, n_parts):
            y = y + _dot(y_refs[p][t["r"], :], w_refs[p][...])
        t["y"] = y

    def query(t):
        t["x1"] = x_ref[t["r"], :] + _rms(t.pop("y"), gpost_ref[...])
        h = _rms(t["x1"], gcpre_ref[...]).astype(BF16)
        t["q"] = (_dot(h, wq_ref[...]) * (hd ** -0.5)).astype(BF16)

    def head(n, t):
        s = _dot(t["q"][:, n * hd:(n + 1) * hd], kt_ref[n * hd:(n + 1) * hd, :])
        e = jnp.exp(s - jnp.max(s, axis=-1, keepdims=True))
        o = _dot(e.astype(BF16), v_ref[:, n * hd:(n + 1) * hd]) / jnp.sum(e, axis=-1, keepdims=True)
        t["o"].append(o.astype(BF16))

    def attn_out(t):
        t["y2"] = _dot(jnp.concatenate(t.pop("o"), axis=-1), wo_ref[...])

    def finish(t):
        xo_ref[t["r"], :] = t.pop("x1") + _rms(t.pop("y2"), gcpost_ref[...])

    phases = ([mix_out, query] + [functools.partial(head, n) for n in range(MEM_HEADS)]
              + [attn_out, finish])
    for phase in phases:
        for t in tiles:
            phase(t)


def _ffn_kernel(x_ref, gpre_ref, wg_ref, wu_ref, wd_ref, g_ref, xo_ref):
    h = _rms(x_ref[...], gpre_ref[...]).astype(BF16)
    acc, pending = None, None

    def down(acc, pending):
        a, c0, cw = pending
        part = _dot(a, wd_ref[c0:c0 + cw, :])
        return part if acc is None else acc + part

    for c0, cw in FFN_CHUNKS:
        g = _dot(h, wg_ref[:, c0:c0 + cw])
        u = _dot(h, wu_ref[:, c0:c0 + cw])
        if pending is not None:
            acc = down(acc, pending)
        pending = ((jax.nn.silu(g) * u).astype(BF16), c0, cw)
    acc = down(acc, pending)
    xo_ref[...] = x_ref[...] + _rms(acc, g_ref[...])


def _params(sem):
    return pltpu.CompilerParams(dimension_semantics=sem, vmem_limit_bytes=VMEM_LIMIT)


def _const_spec(shape, index):
    return pl.BlockSpec(shape, index, pipeline_mode=pl.Buffered(1))


def _layer_vec_spec(row, width):
    return _const_spec((None, 1, width), lambda *_: (row, 0, 0))


def _even_in(x2, vecs, r_pre, w_all, w_t, bf_pad, conv_w, e, tri, pqt, pk, seq, offs):
    n, d_model = x2.shape
    tm = ROW_TILE
    sc_w = offs[2] - offs[1]
    q_rows = pk.shape[1]
    n_heads = q_rows // HEAD_SLOT
    vt_rows = n_heads * VT_SLOT
    tiles_per_seq = seq // tm
    row = lambda i: (i, 0)
    chunked = lambda i: (i // tiles_per_seq, i % tiles_per_seq, 0, 0)
    kern = functools.partial(_even_in_kernel, tiles_per_seq=tiles_per_seq, offs=offs)
    return pl.pallas_call(
        kern,
        grid=(n // tm,),
        in_specs=[
            pl.BlockSpec((tm, d_model), row),
            _layer_vec_spec(r_pre, d_model),
            _const_spec((None, d_model, offs[-1]), lambda i: (e, 0, 0)),
            _const_spec((None, w_t.shape[1], d_model), lambda i: (e, 0, 0)),
            _const_spec((None, 1, LANES), lambda i: (e, 0, 0)),
            _const_spec((None, SC_K, sc_w), lambda i: (e, 0, 0)),
            _const_spec((tm, tm), lambda i: (0, 0)),
            _const_spec(pqt.shape, lambda i: (0, 0)),
            _const_spec((LANES, q_rows), lambda i: (0, 0)),
        ],
        out_specs=[
            pl.BlockSpec((None, tm // ATT_TQ, q_rows, ATT_TQ), chunked),
            pl.BlockSpec((tm, q_rows), row),
            pl.BlockSpec((None, tm // ATT_TK, vt_rows, ATT_TK), chunked),
            pl.BlockSpec((tm, sc_w), row),
        ],
        out_shape=[
            jax.ShapeDtypeStruct((n // seq, seq // ATT_TQ, q_rows, ATT_TQ), BF16),
            jax.ShapeDtypeStruct((n, q_rows), BF16),
            jax.ShapeDtypeStruct((n // seq, seq // ATT_TK, vt_rows, ATT_TK), BF16),
            jax.ShapeDtypeStruct((n, sc_w), BF16),
        ],
        scratch_shapes=[pltpu.VMEM((1, LANES), F32), pltpu.VMEM((SUBLANES, sc_w), F32)],
        compiler_params=_params(("arbitrary",)),
        name="even_in",
    )(x2, vecs, w_all, w_t, bf_pad, conv_w, tri, pqt, pk)


def _fox_attn(qt, ka, vt, seq):
    bsz, nq, _, _ = qt.shape
    _, nkb, vt_rows, _ = vt.shape
    hps = ATT_HEADS_PER_STEP
    n_groups = vt_rows // (hps * VT_SLOT)
    return pl.pallas_call(
        _fox_attn_kernel,
        grid=(bsz, n_groups, nq),
        in_specs=[
            pl.BlockSpec((None, None, hps * HEAD_SLOT, ATT_TQ), lambda b, p, i: (b, i, p, 0)),
            pl.BlockSpec((seq, hps * HEAD_SLOT), lambda b, p, i: (b, p)),
            pl.BlockSpec((None, nkb, hps * VT_SLOT, ATT_TK), lambda b, p, i: (b, 0, p, 0)),
        ],
        out_specs=pl.BlockSpec((ATT_TQ, hps * FOX_HEAD_DIM), lambda b, p, i: (b * nq + i, p)),
        out_shape=jax.ShapeDtypeStruct((bsz * seq, n_groups * hps * FOX_HEAD_DIM), BF16),
        scratch_shapes=[pltpu.VMEM((hps, VT_SLOT, ATT_TQ), F32)],
        compiler_params=_params(("arbitrary", "arbitrary", "arbitrary")),
        name="fox_attn",
    )(qt, ka, vt)


def _lru(x2, vecs, rows, w_in, conv_w, w_a, w_i, o, seq):
    r_pre, r_cb, r_ba, r_bi, r_lam = rows
    n, d_model = x2.shape
    bsz = n // seq
    assert bsz == SUBLANES
    steps = LRU_STEPS
    nb = d_model // LRU_BW
    blk = pl.BlockSpec((bsz, steps, d_model), lambda i: (0, i, 0))
    relayout = pltpu.VMEM((d_model // LANES, bsz * LRU_PITCH, LANES), F32)
    y = pl.pallas_call(
        _lru_kernel,
        grid=(seq // steps,),
        in_specs=[
            blk,
            _layer_vec_spec(r_pre, d_model),
            _const_spec((None, d_model, 2 * d_model), lambda i: (o, 0, 0)),
            _const_spec((None, RG_CONV_K, d_model), lambda i: (o, 0, 0)),
            _layer_vec_spec(r_cb, d_model),
            _const_spec((None, nb, LRU_BW, LRU_BW), lambda i: (o, 0, 0, 0)),
            _layer_vec_spec(r_ba, d_model),
            _const_spec((None, nb, LRU_BW, LRU_BW), lambda i: (o, 0, 0, 0)),
            _layer_vec_spec(r_bi, d_model),
            _layer_vec_spec(r_lam, d_model),
        ],
        out_specs=blk,
        out_shape=jax.ShapeDtypeStruct((bsz, seq, d_model), BF16),
        scratch_shapes=[pltpu.VMEM((bsz, d_model), F32),
                        pltpu.VMEM(((RG_CONV_K - 1) * bsz, d_model), F32),
                        relayout, relayout],
        compiler_params=_params(("arbitrary",)),
        name="rglru",
    )(x2.reshape(bsz, seq, d_model), vecs, w_in, conv_w, vecs, w_a, vecs, w_i, vecs, vecs)
    return y.reshape(n, d_model)


def _mem_kv(mem, vecs, row0, w_xkv):
    bsz, mlen, d_model = mem.shape
    depth = w_xkv.shape[0]
    nb = bsz // MEM_KV_SPLIT
    return pl.pallas_call(
        _mem_kv_kernel,
        grid=(depth, MEM_KV_SPLIT),
        in_specs=[
            pl.BlockSpec((nb, mlen, d_model), lambda l, b: (b, 0, 0)),
            pl.BlockSpec((None, 1, d_model), lambda l, b: (row0 + l, 0, 0)),
            pl.BlockSpec((None, d_model, 2 * d_model), lambda l, b: (l, 0, 0)),
        ],
        out_specs=[
            pl.BlockSpec((None, nb, d_model, mlen), lambda l, b: (l, b, 0, 0)),
            pl.BlockSpec((None, nb, mlen, d_model), lambda l, b: (l, b, 0, 0)),
        ],
        out_shape=[jax.ShapeDtypeStruct((depth, bsz, d_model, mlen), BF16),
                   jax.ShapeDtypeStruct((depth, bsz, mlen, d_model), BF16)],
        compiler_params=_params(("arbitrary", "arbitrary")),
        name="mem_kv",
    )(mem, vecs, w_xkv)


def _cross(y_parts, w_parts, x2, layer, vecs, rows, w_xq, k_mem, v_mem, w_xo, w_gu, w_down,
           seq):
    n, d_model = x2.shape
    d_ff = w_down.shape[1]
    steps = n // CROSS_TILE
    gu_rows, down_rows = d_model // steps, d_ff // steps
    slab = lambda i: (i, 0)
    r_post, r_cpre, r_cpost = rows
    tm = CROSS_TILE
    tiles_per_seq = seq // tm
    mlen = v_mem.shape[2]
    row = lambda i: (i, 0)
    in_specs = [pl.BlockSpec((tm, yp.shape[1]), row) for yp in y_parts]
    w_arrays = []
    for yp, (w_arr, idx, blk) in zip(y_parts, w_parts):
        in_specs.append(_const_spec((None, yp.shape[1], d_model),
                                    lambda i, idx=idx, blk=blk: (idx, blk, 0)))
        w_arrays.append(w_arr)
    mem_index = lambda i: (layer, i // tiles_per_seq, 0, 0)
    in_specs += [
        pl.BlockSpec((tm, d_model), row),
        _layer_vec_spec(r_post, d_model),
        _layer_vec_spec(r_cpre, d_model),
        _const_spec((None, d_model, d_model), lambda i: (layer, 0, 0)),
        pl.BlockSpec((None, None, d_model, mlen), mem_index),
        pl.BlockSpec((None, None, mlen, d_model), mem_index),
        _const_spec((None, d_model, d_model), lambda i: (layer, 0, 0)),
        _layer_vec_spec(r_cpost, d_model),
        pl.BlockSpec((None, gu_rows, 2 * d_ff), lambda i: (layer, i, 0)),
        pl.BlockSpec((None, down_rows, d_model), lambda i: (layer, i, 0)),
    ]
    kern = functools.partial(_cross_kernel, n_parts=len(y_parts))
    return pl.pallas_call(
        kern,
        grid=(steps,),
        in_specs=in_specs,
        out_specs=[pl.BlockSpec((tm, d_model), row),
                   pl.BlockSpec((gu_rows, 2 * d_ff), slab),
                   pl.BlockSpec((down_rows, d_model), slab)],
        out_shape=[jax.ShapeDtypeStruct((n, d_model), F32),
                   jax.ShapeDtypeStruct((d_model, 2 * d_ff), BF16),
                   jax.ShapeDtypeStruct((d_ff, d_model), BF16)],
        compiler_params=_params(("arbitrary",)),
        name="cross",
    )(*y_parts, *w_arrays, x2, vecs, vecs, w_xq, k_mem, v_mem, w_xo, vecs, w_gu, w_down)


def _ffn(x2, w_gu, w_down, vecs, r_pre, r_post):
    n, d_model = x2.shape
    d_ff = w_down.shape[0]
    tm = FFN_TILE
    row = lambda i: (i, 0)
    return pl.pallas_call(
        _ffn_kernel,
        grid=(n // tm,),
        in_specs=[
            pl.BlockSpec((tm, d_model), row),
            _layer_vec_spec(r_pre, d_model),
            _const_spec((d_model, d_ff), lambda i: (0, 0)),
            _const_spec((d_model, d_ff), lambda i: (0, 1)),
            _const_spec((d_ff, d_model), lambda i: (0, 0)),
            _layer_vec_spec(r_post, d_model),
        ],
        out_specs=pl.BlockSpec((tm, d_model), row),
        out_shape=jax.ShapeDtypeStruct((n, d_model), F32),
        compiler_params=_params(("arbitrary",)),
        name="ffn",
    )(x2, vecs, w_gu, w_gu, w_down, vecs)


def _pad_heads(w, n_heads, slot):
    e, d, _ = w.shape
    w = w.reshape(e, d, n_heads, FOX_HEAD_DIM)
    w = jnp.pad(w, ((0, 0), (0, 0), (0, 0), (0, slot - FOX_HEAD_DIM)))
    return w.reshape(e, d, n_heads * slot)


def _bias_placement(n_heads):
    pad_rows = VT_SLOT - FOX_HEAD_DIM
    pqt = [[0.0] * LANES for _ in range(n_heads * pad_rows)]
    pk = [[0.0] * (n_heads * HEAD_SLOT) for _ in range(LANES)]
    one_lane = 24
    for h in range(n_heads):
        base = h * HEAD_SLOT + FOX_HEAD_DIM
        for part in range(3):
            pqt[h * pad_rows + part][part * 8 + h] = 1.0
            pk[one_lane][base + part] = 1.0
            pqt[h * pad_rows + 3 + part][one_lane] = 1.0
            pk[part * 8 + h][base + 3 + part] = -1.0
    return jnp.array(pqt, BF16), jnp.array(pk, BF16)


def kernel(x, mem, g_mix_pre, g_mix_post, g_cross_pre, g_mem, g_cross_post, g_ffn_pre, g_ffn_post, w_xq, w_xkv, w_xo, w_ffn_gu, w_ffn_down, ab_w_in, ab_b_f, ab_conv_w, ab_w_out, c_w_in, c_conv_w, c_conv_b, c_w_a, c_b_a, c_w_i, c_b_i, c_lam, c_w_out):
    bsz, seq, d_model = x.shape
    depth = g_mix_pre.shape[0]
    n = bsz * seq
    n_heads = ab_b_f.shape[1]
    fox_w = n_heads * FOX_HEAD_DIM
    sc_w = ab_conv_w.shape[2]
    assert n_heads <= 8 and seq % ROW_TILE == 0 and seq % LRU_STEPS == 0 and seq % ATT_TQ == 0

    n_odd = c_lam.shape[0]
    gains = (g_mix_pre, g_mix_post, g_cross_pre, g_mem, g_cross_post, g_ffn_pre, g_ffn_post)
    odd_vecs = (c_conv_b, 0.5 * c_b_a.reshape(n_odd, -1), 0.5 * c_b_i.reshape(n_odd, -1), c_lam)
    vecs = jnp.concatenate(gains + odd_vecs, axis=0)[:, None, :]
    (R_MIX_PRE, R_MIX_POST, R_CROSS_PRE, R_MEM, R_CROSS_POST, R_FFN_PRE,
     R_FFN_POST) = (k * depth for k in range(len(gains)))
    R_CONV_B, R_B_A, R_B_I, R_LAM = (len(gains) * depth + k * n_odd for k in range(4))

    i1, i2, i3 = fox_w, 2 * fox_w, 3 * fox_w
    i4 = i3 + n_heads
    i5, i6 = i4 + sc_w, i4 + 2 * sc_w
    w_in_b = ab_w_in.astype(BF16)
    w_f = w_in_b[:, :, i3:i4]
    w_k = _pad_heads(w_in_b[:, :, i1:i2], n_heads, HEAD_SLOT)
    w_k = w_k.at[:, :, F_LANE0:F_LANE0 + 3 * n_heads].set(jnp.concatenate([w_f, w_f, w_f], -1))
    pieces = [w_k, w_in_b[:, :, i4:i5], w_in_b[:, :, i5:i6], w_in_b[:, :, i6:]]
    w_t = jnp.concatenate([w_in_b[:, :, 0:i1], w_in_b[:, :, i2:i3]], axis=-1)
    w_t = w_t.transpose(0, 2, 1)
    offs = []
    for p in pieces:
        offs.append((offs[-1] if offs else 0) + p.shape[-1])
    offs = tuple(offs)
    w_even = jnp.concatenate(pieces, axis=-1)
    b_f3 = jnp.pad(jnp.concatenate([ab_b_f, ab_b_f, ab_b_f], axis=-1),
                   ((0, 0), (0, LANES - 3 * n_heads))).reshape(-1, 1, LANES)
    tri = (lax.broadcasted_iota(jnp.int32, (ROW_TILE, ROW_TILE), 0)
           >= lax.broadcasted_iota(jnp.int32, (ROW_TILE, ROW_TILE), 1)).astype(BF16)
    pqt, pk = _bias_placement(n_heads)

    w_out_even = ab_w_out.astype(BF16)
    w_c_in = c_w_in.astype(BF16)
    w_c_out = c_w_out.astype(BF16)
    w_a = (0.5 * c_w_a).astype(BF16)
    w_i = (0.5 * c_w_i).astype(BF16)
    w_xq_b = w_xq.astype(BF16)
    w_xo_b = w_xo.astype(BF16)

    k_mem, v_mem = _mem_kv(mem, vecs, R_MEM, w_xkv)

    x2 = x.reshape(n, d_model)
    for layer in range(depth):
        if layer % 2 == 0:
            e = layer // 2
            qt, ka, vt, yb = _even_in(x2, vecs, R_MIX_PRE + layer, w_even, w_t, b_f3,
                                      ab_conv_w, e, tri, pqt, pk, seq, offs)
            ya = _fox_attn(qt, ka, vt, seq)
            y_parts = [ya, yb]
            assert fox_w == sc_w
            w_parts = [(w_out_even, e, 0), (w_out_even, e, 1)]
        else:
            o = layer // 2
            rows = (R_MIX_PRE + layer, R_CONV_B + o, R_B_A + o, R_B_I + o, R_LAM + o)
            ym = _lru(x2, vecs, rows, w_c_in, c_conv_w, w_a, w_i, o, seq)
            y_parts = [ym]
            w_parts = [(w_c_out, o, 0)]
        rows = (R_MIX_POST + layer, R_CROSS_PRE + layer, R_CROSS_POST + layer)
        x2, w_gu_b, w_down_b = _cross(y_parts, w_parts, x2, layer, vecs, rows, w_xq_b,
                                      k_mem, v_mem, w_xo_b, w_ffn_gu, w_ffn_down, seq)
        x2 = _ffn(x2, w_gu_b, w_down_b, vecs, R_FFN_PRE + layer, R_FFN_POST + layer)
    return x2.reshape(bsz, seq, d_model)
```
